```python
import math
import jax, jax.numpy as jnp
from jax import lax
import numpy as np

D_MODEL = 1024
BATCH = 8
SEQ = 4096
DEPTH = 2

HEAD_DIM = 64
N_HEADS = 4
MIX_WIDTH = N_HEADS * HEAD_DIM
N_MIXERS = 4
ROPE_THETA = 500000.0
ROPE_FRACTION_DEN = 4
Q_BLOCK = 128
SPARSE_Q_BLOCK = 32
DIFF_DIM = HEAD_DIM // 2
MOBA_BLOCK = 256
MOBA_TOPK = 3
NSA_CMP_LEN = 32
NSA_CMP_STRIDE = 16
NSA_SLC_BLOCK = 64
NSA_TOPN = 16
NSA_WINDOW = 512
NSA_KV_DIM = HEAD_DIM
FFN_HIDDEN = -(-8 * D_MODEL // (3 * 256)) * 256
NEG_INF = -1e30
BIG = 1e30
RMS_EPS = 1e-6
IN_SPLITS = (MIX_WIDTH,) * 10 + (NSA_KV_DIM,) * 6 + (3 * N_HEADS,)
IN_COLS = sum(IN_SPLITS)

kernel_name = "hybrid_gated_sparse_attention_trunk"


def _rms_norm(x, g):
    xf = x.astype(jnp.float32)
    y = xf * lax.rsqrt(jnp.mean(xf * xf, axis=-1, keepdims=True) + RMS_EPS)
    return (y * g.astype(jnp.float32)).astype(x.dtype)


def _rope(x, pos):
    d = x.shape[-1]
    r = d // ROPE_FRACTION_DEN
    half = r // 2
    inv = ROPE_THETA ** (-jnp.arange(half, dtype=jnp.float32) * 2.0 / r)
    ang = pos.astype(jnp.float32)[:, None] * inv[None, :]
    cos, sin = jnp.cos(ang), jnp.sin(ang)
    xf = x.astype(jnp.float32)
    x1, x2 = xf[..., :half], xf[..., half:r]
    out = jnp.concatenate([x1 * cos - x2 * sin, x2 * cos + x1 * sin, xf[..., r:]], axis=-1)
    return out.astype(x.dtype)


def _heads(t):
    b, s, _ = t.shape
    return t.reshape(b, s, N_HEADS, -1).transpose(0, 2, 1, 3)


def _merge_heads(o):
    b, h, s, d = o.shape
    return o.transpose(0, 2, 1, 3).reshape(b, s, h * d)


def _masked_softmax(s, mask):
    s = jnp.where(mask, s.astype(jnp.float32), NEG_INF)
    m = jnp.max(s, axis=-1, keepdims=True)
    e = jnp.where(mask, jnp.exp(s - m), 0.0)
    return e / jnp.maximum(jnp.sum(e, axis=-1, keepdims=True), 1e-30)


def _sweep(fn, seq, blk):
    return lax.map(fn, jnp.arange(seq // blk, dtype=jnp.int32) * blk)


def _unblock(y):
    n, b, h, c, d = y.shape
    return jnp.transpose(y, (1, 2, 0, 3, 4)).reshape(b, h, n * c, d)


def _stick_breaking(q, k, v):
    b, h, s, d = q.shape
    scale = 1.0 / math.sqrt(d)
    kpos = jnp.arange(s, dtype=jnp.int32)

    def blk(c0):
        qb = lax.dynamic_slice_in_dim(q, c0, Q_BLOCK, axis=2)
        z = jnp.einsum('bhqd,bhkd->bhqk', qb, k).astype(jnp.float32) * scale
        qpos = c0 + jnp.arange(Q_BLOCK, dtype=jnp.int32)
        mask = kpos[None, :] < qpos[:, None]
        log1m = jnp.where(mask, jax.nn.log_sigmoid(-z), 0.0)
        between = lax.cumsum(log1m, axis=3, reverse=True) - log1m
        a = jnp.where(mask, jnp.exp(jax.nn.log_sigmoid(z) + between), 0.0)
        return jnp.einsum('bhqk,bhkd->bhqd', a.astype(v.dtype), v)

    return _unblock(_sweep(blk, s, Q_BLOCK))


def _diff_attention(q1, q2, k1, k2, v, lam):
    b, h, s, d = q1.shape
    scale = 1.0 / math.sqrt(d)
    kpos = jnp.arange(s, dtype=jnp.int32)

    def blk(c0):
        qpos = c0 + jnp.arange(Q_BLOCK, dtype=jnp.int32)
        mask = kpos[None, :] <= qpos[:, None]
        q1b = lax.dynamic_slice_in_dim(q1, c0, Q_BLOCK, axis=2)
        q2b = lax.dynamic_slice_in_dim(q2, c0, Q_BLOCK, axis=2)
        p1 = _masked_softmax(jnp.einsum('bhqd,bhkd->bhqk', q1b, k1).astype(jnp.float32) * scale, mask)
        p2 = _masked_softmax(jnp.einsum('bhqd,bhkd->bhqk', q2b, k2).astype(jnp.float32) * scale, mask)
        p = p1 - lam * p2
        return jnp.einsum('bhqk,bhkd->bhqd', p.astype(v.dtype), v)

    return _unblock(_sweep(blk, s, Q_BLOCK))


def _moba(q, k, v):
    b, h, s, d = q.shape
    scale = 1.0 / math.sqrt(d)
    nb = -(-s // MOBA_BLOCK)
    pad = nb * MOBA_BLOCK - s
    kb = jnp.pad(k, ((0, 0), (0, 0), (0, pad), (0, 0))).reshape(b, h, nb, MOBA_BLOCK, d)
    vb = jnp.pad(v, ((0, 0), (0, 0), (0, pad), (0, 0))).reshape(b, h, nb, MOBA_BLOCK, d)
    kmean = jnp.mean(kb.astype(jnp.float32), axis=3).astype(k.dtype)
    ksel = min(MOBA_TOPK, nb - 1)
    bi = jnp.arange(b)[:, None, None, None]
    hi = jnp.arange(h)[None, :, None, None]
    blk_ids = jnp.arange(nb, dtype=jnp.int32)
    off = jnp.arange(MOBA_BLOCK, dtype=jnp.int32)
    L = MOBA_BLOCK
    C = SPARSE_Q_BLOCK

    def blk(c0):
        qb = lax.dynamic_slice_in_dim(q, c0, C, axis=2)
        qpos = c0 + jnp.arange(C, dtype=jnp.int32)
        ob = c0 // MOBA_BLOCK
        k_own = lax.dynamic_index_in_dim(kb, ob, axis=2, keepdims=False)
        v_own = lax.dynamic_index_in_dim(vb, ob, axis=2, keepdims=False)
        s_own = jnp.einsum('bhqd,bhkd->bhqk', qb, k_own).astype(jnp.float32) * scale
        m_own = jnp.broadcast_to((ob * MOBA_BLOCK + off)[None, :] <= qpos[:, None], s_own.shape)
        if ksel == 0:
            p_own = _masked_softmax(s_own, m_own).astype(v.dtype)
            return jnp.einsum('bhqk,bhkd->bhqd', p_own, v_own)
        gate = jnp.einsum('bhqd,bhnd->bhqn', qb, kmean).astype(jnp.float32)
        gate = jnp.where(blk_ids < ob, gate, NEG_INF)
        _, idx = lax.top_k(gate, ksel)
        valid = idx < ob
        k_sel = kb[bi, hi, idx]
        v_sel = vb[bi, hi, idx]
        s_sel = jnp.einsum('bhqd,bhqnkd->bhqnk', qb, k_sel).astype(jnp.float32) * scale
        m_sel = jnp.broadcast_to(valid[..., None], s_sel.shape)
        sc = jnp.concatenate([s_sel.reshape(b, h, C, ksel * L), s_own], axis=-1)
        mk = jnp.concatenate([m_sel.reshape(b, h, C, ksel * L), m_own], axis=-1)
        p = _masked_softmax(sc, mk).astype(v.dtype)
        p_sel = p[..., :ksel * L].reshape(b, h, C, ksel, L)
        p_own = p[..., ksel * L:]
        return (jnp.einsum('bhqnk,bhqnkd->bhqd', p_sel, v_sel)
                + jnp.einsum('bhqk,bhkd->bhqd', p_own, v_own))

    return _unblock(_sweep(blk, s, C))


def _nsa(q, kc_tok, vc_tok, ks, vs, kw, vw, gates, kn_g, cmp_pe, cmp_w):
    b, h, s, d = q.shape
    scale = 1.0 / math.sqrt(d)
    pos = jnp.arange(s, dtype=jnp.int32)
    C = SPARSE_Q_BLOCK
    nc = (s - NSA_CMP_LEN) // NSA_CMP_STRIDE + 1
    cstart = NSA_CMP_STRIDE * jnp.arange(nc, dtype=jnp.int32)
    widx = cstart[:, None] + jnp.arange(NSA_CMP_LEN, dtype=jnp.int32)[None, :]
    cmp_end = cstart + (NSA_CMP_LEN - 1)

    def compress(t, pe, w):
        blocks = t[:, widx] + pe
        return blocks.reshape(b, nc, NSA_CMP_LEN * d) @ w

    kc = _rope(_rms_norm(compress(kc_tok, cmp_pe[0], cmp_w[0]), kn_g[0]), cmp_end)
    vc = compress(vc_tok, cmp_pe[1], cmp_w[1])
    ns = s // NSA_SLC_BLOCK
    ntop = min(NSA_TOPN, ns)
    ks = _rope(_rms_norm(ks, kn_g[1]), pos)
    ksb = ks.reshape(b, ns, NSA_SLC_BLOCK, d)
    vsb = vs.reshape(b, ns, NSA_SLC_BLOCK, d)
    sstart = NSA_SLC_BLOCK * jnp.arange(ns, dtype=jnp.int32)
    overlap = jnp.clip(jnp.minimum(cstart[:, None] + NSA_CMP_LEN, sstart[None, :] + NSA_SLC_BLOCK)
                       - jnp.maximum(cstart[:, None], sstart[None, :]), 0, None)
    cmp_to_slc = overlap.astype(jnp.float32) / NSA_CMP_STRIDE
    slc_ids = jnp.arange(ns, dtype=jnp.int32)
    slc_off = jnp.arange(NSA_SLC_BLOCK, dtype=jnp.int32)
    bi = jnp.arange(b)[:, None, None]
    kw = _rope(_rms_norm(kw, kn_g[2]), pos)
    kw_pad = jnp.pad(kw, ((0, 0), (NSA_WINDOW, 0), (0, 0)))
    vw_pad = jnp.pad(vw, ((0, 0), (NSA_WINDOW, 0), (0, 0)))
    woff = jnp.arange(NSA_WINDOW + C, dtype=jnp.int32)

    def blk(c0):
        qb = lax.dynamic_slice_in_dim(q, c0, C, axis=2)
        qpos = c0 + jnp.arange(C, dtype=jnp.int32)
        s_c = jnp.einsum('bhqd,bnd->bhqn', qb, kc).astype(jnp.float32) * scale
        p_c = _masked_softmax(s_c, cmp_end[None, :] <= qpos[:, None])
        o_c = jnp.einsum('bhqn,bnd->bhqd', p_c.astype(vc.dtype), vc)
        imp = jnp.einsum('bhqn,nj->bqj', p_c, cmp_to_slc)
        cur = qpos // NSA_SLC_BLOCK
        forced = ((slc_ids[None, :] == 0) | (slc_ids[None, :] == cur[:, None])
                  | (slc_ids[None, :] == cur[:, None] - 1))
        allowed = slc_ids[None, :] <= cur[:, None]
        score = jnp.where(allowed, jnp.where(forced, BIG, imp), NEG_INF)
        _, idx = lax.top_k(score, ntop)
        k_sel = ksb[bi, idx]
        v_sel = vsb[bi, idx]
        kpos_sel = idx[..., None] * NSA_SLC_BLOCK + slc_off
        m_s = (kpos_sel <= qpos[None, :, None, None]).reshape(b, 1, C, ntop * NSA_SLC_BLOCK)
        s_s = jnp.einsum('bhqd,bqnkd->bhqnk', qb, k_sel).astype(jnp.float32) * scale
        p_s = _masked_softmax(s_s.reshape(b, h, C, ntop * NSA_SLC_BLOCK), m_s)
        p_s = p_s.reshape(b, h, C, ntop, NSA_SLC_BLOCK).astype(v_sel.dtype)
        o_s = jnp.einsum('bhqnk,bqnkd->bhqd', p_s, v_sel)
        kwb = lax.dynamic_slice_in_dim(kw_pad, c0, NSA_WINDOW + C, axis=1)
        vwb = lax.dynamic_slice_in_dim(vw_pad, c0, NSA_WINDOW + C, axis=1)
        kp = c0 - NSA_WINDOW + woff
        m_w = ((kp[None, :] <= qpos[:, None]) & (kp[None, :] > qpos[:, None] - NSA_WINDOW)
               & (kp[None, :] >= 0))
        s_w = jnp.einsum('bhqd,bkd->bhqk', qb, kwb).astype(jnp.float32) * scale
        o_w = jnp.einsum('bhqk,bkd->bhqd', _masked_softmax(s_w, m_w).astype(vwb.dtype), vwb)
        return (o_c, o_s, o_w)

    o_c, o_s, o_w = _sweep(blk, s, C)
    return gates[0] * _unblock(o_c) + gates[1] * _unblock(o_s) + gates[2] * _unblock(o_w)


def setup_inputs(seed: int = 0) -> dict:
    key = jax.random.key(seed)
    ks = jax.random.split(key, 21)
    D, L = D_MODEL, DEPTH

    def nrm(k, shape, scale):
        return jax.random.normal(k, shape, jnp.float32) * scale

    def gain(k, shape):
        return 1.0 + nrm(k, shape, 0.05)

    return {
        "x": nrm(ks[0], (BATCH, SEQ, D), 1.0),
        "attn_norm_g": gain(ks[1], (L, D)),
        "w_in": nrm(ks[2], (L, D, IN_COLS), D ** -0.5),
        "diff_qn_g": gain(ks[3], (L, DIFF_DIM)),
        "diff_kn_g": gain(ks[4], (L, DIFF_DIM)),
        "diff_lam": nrm(ks[5], (L, 4, DIFF_DIM), 0.1),
        "diff_subln_g": gain(ks[6], (L, HEAD_DIM)),
        "moba_qn_g": gain(ks[7], (L, HEAD_DIM)),
        "moba_kn_g": gain(ks[8], (L, HEAD_DIM)),
        "nsa_qn_g": gain(ks[9], (L, HEAD_DIM)),
        "nsa_kn_g": gain(ks[10], (L, 3, NSA_KV_DIM)),
        "nsa_cmp_pe": nrm(ks[11], (L, 2, NSA_CMP_LEN, NSA_KV_DIM), 0.1),
        "nsa_cmp_w": nrm(ks[12], (L, 2, NSA_CMP_LEN * NSA_KV_DIM, NSA_KV_DIM), (NSA_CMP_LEN * NSA_KV_DIM) ** -0.5),
        "w_gate": nrm(ks[13], (L, N_MIXERS, D, D), D ** -0.5),
        "b_gate": nrm(ks[14], (L, N_MIXERS, D), 0.01),
        "w_branch": nrm(ks[15], (L, N_MIXERS, MIX_WIDTH, D), MIX_WIDTH ** -0.5),
        "w_out": nrm(ks[16], (L, D, D), D ** -0.5),
        "ffn_norm_g": gain(ks[17], (L, D)),
        "w_ffn_gate": nrm(ks[18], (L, D, FFN_HIDDEN), D ** -0.5),
        "w_ffn_up": nrm(ks[19], (L, D, FFN_HIDDEN), D ** -0.5),
        "w_ffn_down": nrm(ks[20], (L, FFN_HIDDEN, D), FFN_HIDDEN ** -0.5),
    }


def reference(x, attn_norm_g, w_in, diff_qn_g, diff_kn_g, diff_lam, diff_subln_g,
              moba_qn_g, moba_kn_g, nsa_qn_g, nsa_kn_g, nsa_cmp_pe, nsa_cmp_w,
              w_gate, b_gate, w_branch, w_out, ffn_norm_g, w_ffn_gate, w_ffn_up, w_ffn_down):
    b, s, _ = x.shape
    pos = jnp.arange(s, dtype=jnp.int32)
    split_at = [int(o) for o in np.cumsum(IN_SPLITS)[:-1]]
    for l in range(DEPTH):
        xn = _rms_norm(x, attn_norm_g[l])
        (sb_q, sb_k, sb_v, df_q, df_k, df_v, mb_q, mb_k, mb_v, ns_q,
         ns_kc, ns_vc, ns_ks, ns_vs, ns_kw, ns_vw, ns_g) = jnp.split(xn @ w_in[l], split_at, axis=-1)

        o_a = _stick_breaking(_heads(sb_q), _heads(sb_k), _heads(sb_v))

        lam_init = 0.8 - 0.6 * math.exp(-0.3 * l)
        dq = df_q.reshape(b, s, N_HEADS, 2, DIFF_DIM).transpose(3, 0, 2, 1, 4)
        dk = df_k.reshape(b, s, N_HEADS, 2, DIFF_DIM).transpose(3, 0, 2, 1, 4)
        dq = _rope(_rms_norm(dq, diff_qn_g[l]), pos)
        dk = _rope(_rms_norm(dk, diff_kn_g[l]), pos)
        lp = diff_lam[l].astype(jnp.float32)
        lam = jnp.exp(jnp.sum(lp[0] * lp[1])) - jnp.exp(jnp.sum(lp[2] * lp[3])) + lam_init
        o_b = _diff_attention(dq[0], dq[1], dk[0], dk[1], _heads(df_v), lam)
        o_b = _rms_norm(o_b, diff_subln_g[l]) * (1.0 - lam_init)

        mq = _rope(_rms_norm(_heads(mb_q), moba_qn_g[l]), pos)
        mk = _rope(_rms_norm(_heads(mb_k), moba_kn_g[l]), pos)
        o_c = _moba(mq, mk, _heads(mb_v))

        nq = _rope(_rms_norm(_heads(ns_q), nsa_qn_g[l]), pos)
        g_nsa = jax.nn.sigmoid(ns_g).reshape(b, s, 3, N_HEADS).transpose(2, 0, 3, 1)[..., None]
        o_d = _nsa(nq, ns_kc, ns_vc, ns_ks, ns_vs, ns_kw, ns_vw, g_nsa,
                   nsa_kn_g[l], nsa_cmp_pe[l], nsa_cmp_w[l])

        merged = None
        for i, o in enumerate((o_a, o_b, o_c, o_d)):
            g = jax.nn.sigmoid(xn @ w_gate[l, i] + b_gate[l, i])
            term = g * (_merge_heads(o) @ w_branch[l, i])
            merged = term if merged is None else merged + term
        x = x + merged @ w_out[l]

        hn = _rms_norm(x, ffn_norm_g[l])
        x = x + (jax.nn.silu(hn @ w_ffn_gate[l]) * (hn @ w_ffn_up[l])) @ w_ffn_down[l]
    return x
```

```python
import functools
import math

import numpy as np
import jax
import jax.numpy as jnp
from jax import lax
from jax.experimental import pallas as pl
from jax.experimental.pallas import tpu as pltpu

HEAD_DIM = 64
N_HEADS = 4
MIX_WIDTH = N_HEADS * HEAD_DIM
ROPE_THETA = 500000.0
ROPE_FRACTION_DEN = 4
DIFF_DIM = HEAD_DIM // 2
MOBA_BLOCK = 256
MOBA_TOPK = 3
NSA_CMP_LEN = 32
NSA_CMP_STRIDE = 16
NSA_SLC_BLOCK = 64
NSA_TOPN = 16
NSA_WINDOW = 512
NEG_INF = -1e30
BIG = 1e30
RMS_EPS = 1e-6

LANES = 128
ATT_TILE = 256
GROUP = 256
N_GROUPS = 12
VMEM_LIMIT = 56 * 1024 * 1024

F32 = jnp.float32
BF16 = jnp.bfloat16

_NT = (((1,), (1,)), ((), ()))


def _dot(a, b):
    return jnp.dot(a, b, preferred_element_type=F32)


def _dot_nt(a, b):
    return lax.dot_general(a, b, _NT, preferred_element_type=F32)


def _split_bf16(x):
    hi = x.astype(BF16)
    lo = (x - hi.astype(F32)).astype(BF16)
    return hi, lo


def _params(*sem):
    return pltpu.CompilerParams(dimension_semantics=sem, vmem_limit_bytes=VMEM_LIMIT)


def _resident(shape, index_map):
    return pl.BlockSpec(shape, index_map, pipeline_mode=pl.Buffered(1))


def _rope_tables(pos, d, width):
    r = d // ROPE_FRACTION_DEN
    half = r // 2
    inv = ROPE_THETA ** (-jnp.arange(half, dtype=F32) * 2.0 / r)
    ang = pos.astype(F32)[:, None] * inv[None, :]
    cos, sin = jnp.cos(ang), jnp.sin(ang)
    n = pos.shape[0]
    one = jnp.ones((n, d - r), F32)
    zero_h = jnp.zeros((n, half), F32)
    zero_t = jnp.zeros((n, d - r), F32)
    cos_g = jnp.concatenate([cos, cos, one], axis=1)
    sa_g = jnp.concatenate([-sin, zero_h, zero_t], axis=1)
    sb_g = jnp.concatenate([zero_h, sin, zero_t], axis=1)
    reps = width // d
    return (jnp.tile(cos_g, (1, reps)), jnp.tile(sa_g, (1, reps)), jnp.tile(sb_g, (1, reps)))


def _group_mean_matrix(d, width):
    g = np.arange(width) // d
    return jnp.asarray((g[:, None] == g[None, :]).astype(np.float32) / d, BF16)


def _rope_apply(y, cos, sa, sb, half):
    w = y.shape[-1]
    return y * cos + pltpu.roll(y, w - half, 1) * sa + pltpu.roll(y, half, 1) * sb


_GROUP_CFG = (
    (None, None), (None, None), (None, None),
    (DIFF_DIM, DIFF_DIM), (DIFF_DIM, DIFF_DIM), (None, None),
    (HEAD_DIM, HEAD_DIM), (HEAD_DIM, HEAD_DIM), (None, None),
    (HEAD_DIM, HEAD_DIM),
    (HEAD_DIM, HEAD_DIM),
    (HEAD_DIM, HEAD_DIM),
)


def _inproj_kernel(x_ref, g_ref, w_ref, gain_ref, nflag_ref, rflag_ref, b64_ref, b32_ref,
                   c64_ref, sa64_ref, sb64_ref, c32_ref, sa32_ref, sb32_ref, o_ref):
    x = x_ref[...]
    xn = x * lax.rsqrt(jnp.mean(x * x, axis=-1, keepdims=True) + RMS_EPS) * g_ref[...]
    xn = xn.astype(BF16)
    for gi, (norm, rope) in enumerate(_GROUP_CFG):
        cols = slice(gi * GROUP, (gi + 1) * GROUP)
        y = _dot(xn, w_ref[:, cols])
        gain = gain_ref[gi]
        if norm is not None:
            bmat = b64_ref[...] if norm == HEAD_DIM else b32_ref[...]
            ms = _dot((y * y).astype(BF16), bmat)
            y = y * jnp.where(nflag_ref[gi] > 0, lax.rsqrt(ms + RMS_EPS) * gain, gain)
        else:
            y = y * gain
        if rope is not None:
            if rope == HEAD_DIM:
                cos, sa, sb = c64_ref[...], sa64_ref[...], sb64_ref[...]
            else:
                cos, sa, sb = c32_ref[...], sa32_ref[...], sb32_ref[...]
            rf = rflag_ref[gi]
            cos = jnp.where(rf > 0, cos, 1.0)
            half = rope // ROPE_FRACTION_DEN // 2
            y = _rope_apply(y, cos, sa * rf, sb * rf, half)
        o_ref[:, cols] = y.astype(o_ref.dtype)


def _inproj(x2, g, w, gain, nflag, rflag, tabs64, tabs32, seq, tm):
    n, d = x2.shape
    ncols = N_GROUPS * GROUP
    nst = seq // tm
    row = lambda i: (i, 0)
    fix2 = lambda i: (0, 0)
    fix3 = lambda i: (0, 0, 0)
    tab = lambda i: (i % nst, 0)
    vec = _resident((N_GROUPS, 1, GROUP), fix3)
    tspec = pl.BlockSpec((tm, GROUP), tab)
    return pl.pallas_call(
        _inproj_kernel,
        grid=(n // tm,),
        in_specs=[pl.BlockSpec((tm, d), row), _resident((1, d), fix2), _resident((d, ncols), fix2),
                  vec, vec, vec, _resident((GROUP, GROUP), fix2), _resident((GROUP, GROUP), fix2),
                  tspec, tspec, tspec, tspec, tspec, tspec],
        out_specs=pl.BlockSpec((tm, ncols), row),
        out_shape=jax.ShapeDtypeStruct((n, ncols), BF16),
        compiler_params=_params("parallel"),
        name="inproj",
    )(x2, g, w, gain, nflag, rflag, _group_mean_matrix(HEAD_DIM, GROUP),
      _group_mean_matrix(DIFF_DIM, GROUP), *tabs64, *tabs32)


def _sb_kernel(q_ref, k_ref, v_ref, o_ref):
    t = ATT_TILE
    i = pl.program_id(1)
    q = q_ref[...]
    r = lax.broadcasted_iota(jnp.int32, (t, t), 0)
    c = lax.broadcasted_iota(jnp.int32, (t, t), 1)
    suffix = (r > c).astype(BF16)
    strictly_past = c < r

    def tile(j, acc, carry, diag):
        start = pl.multiple_of(j * t, t)
        k = k_ref[pl.ds(start, t), :]
        v = v_ref[pl.ds(start, t), :]
        z = _dot_nt(q, k)
        sp = jnp.maximum(z, 0.0) + jnp.log(1.0 + jnp.exp(-jnp.abs(z)))
        l1m = -sp
        if diag:
            l1m = jnp.where(strictly_past, l1m, 0.0)
        hi, lo = _split_bf16(l1m)
        between = _dot(hi, suffix) + _dot(lo, suffix) + carry
        a = jnp.exp(z - sp + between)
        if diag:
            a = jnp.where(strictly_past, a, 0.0)
        acc = acc + _dot(a.astype(BF16), v)
        carry = carry + jnp.sum(l1m, axis=1, keepdims=True)
        return acc, carry

    acc, carry = tile(i, jnp.zeros((t, HEAD_DIM), F32), jnp.zeros((t, 1), F32), True)

    def body(jj, st):
        return tile(i - jj, st[0], st[1], False)

    acc, carry = lax.fori_loop(1, i + 1, body, (acc, carry))
    o_ref[...] = acc.astype(o_ref.dtype)


def _head_specs(base, nheads, seq, width):
    t = ATT_TILE
    qs = pl.BlockSpec((None, None, t, width), lambda bh, i: (base + bh % nheads, bh // nheads, i, 0))
    full = lambda b0: pl.BlockSpec((None, None, seq, width),
                                   lambda bh, i: (b0 + bh % nheads, bh // nheads, 0, 0))
    return qs, full


def _out_spec(nheads):
    return pl.BlockSpec((None, None, ATT_TILE, HEAD_DIM), lambda bh, i: (bh // nheads, bh % nheads, i, 0))


def _stick_breaking(ht, batch, seq):
    qs, full = _head_specs(0, N_HEADS, seq, HEAD_DIM)
    return pl.pallas_call(
        _sb_kernel,
        grid=(batch * N_HEADS, seq // ATT_TILE),
        in_specs=[qs, full(4), full(8)],
        out_specs=_out_spec(N_HEADS),
        out_shape=jax.ShapeDtypeStruct((batch, N_HEADS, seq, HEAD_DIM), BF16),
        compiler_params=_params("parallel", "arbitrary"),
        name="stick_breaking",
    )(ht, ht, ht)


def _online_softmax_step(s, v, m, l, acc):
    mn = jnp.maximum(m, jnp.max(s, axis=1, keepdims=True))
    p = jnp.exp(s - mn)
    alpha = jnp.exp(m - mn)
    l = alpha * l + jnp.sum(p, axis=1, keepdims=True)
    acc = alpha * acc + _dot(p.astype(BF16), v)
    return mn, l, acc


def _diff_kernel(q_ref, k_ref, v_ref, lam_ref, g_ref, o_ref, *, lam_init):
    t = ATT_TILE
    i = pl.program_id(1)
    q = q_ref[...]
    lane = lax.broadcasted_iota(jnp.int32, q.shape, 1)
    zero = jnp.zeros_like(q)
    q1 = jnp.where(lane < DIFF_DIM, q, zero)
    q2 = jnp.where(lane >= DIFF_DIM, q, zero)
    r = lax.broadcasted_iota(jnp.int32, (t, t), 0)
    c = lax.broadcasted_iota(jnp.int32, (t, t), 1)
    causal = c <= r

    def tile(j, st, diag):
        start = pl.multiple_of(j * t, t)
        k = k_ref[pl.ds(start, t), :]
        v = v_ref[pl.ds(start, t), :]
        s1 = _dot_nt(q1, k)
        s2 = _dot_nt(q2, k)
        if diag:
            s1 = jnp.where(causal, s1, NEG_INF)
            s2 = jnp.where(causal, s2, NEG_INF)
        m1, l1, a1, m2, l2, a2 = st
        m1, l1, a1 = _online_softmax_step(s1, v, m1, l1, a1)
        m2, l2, a2 = _online_softmax_step(s2, v, m2, l2, a2)
        return (m1, l1, a1, m2, l2, a2)

    neg = jnp.full((t, 1), NEG_INF, F32)
    z1 = jnp.zeros((t, 1), F32)
    za = jnp.zeros((t, HEAD_DIM), F32)
    st = tile(i, (neg, z1, za, neg, z1, za), True)
    st = lax.fori_loop(0, i, lambda j, s: tile(j, s, False), st)
    _, l1, a1, _, l2, a2 = st

    lp = lam_ref[...]
    lam = (jnp.exp(jnp.sum(lp[0:1, :] * lp[1:2, :], axis=1, keepdims=True))
           - jnp.exp(jnp.sum(lp[2:3, :] * lp[3:4, :], axis=1, keepdims=True)) + lam_init)
    o = a1 / l1 - lam * (a2 / l2)
    o = o * lax.rsqrt(jnp.mean(o * o, axis=-1, keepdims=True) + RMS_EPS) * g_ref[...]
    o_ref[...] = (o * (1.0 - lam_init)).astype(o_ref.dtype)


def _diff_attention(ht, lam_p, subln_g, batch, seq, lam_init):
    qs, full = _head_specs(12, N_HEADS, seq, HEAD_DIM)
    fix = lambda bh, i: (0, 0)
    return pl.pallas_call(
        functools.partial(_diff_kernel, lam_init=lam_init),
        grid=(batch * N_HEADS, seq // ATT_TILE),
        in_specs=[qs, full(16), full(20),
                  pl.BlockSpec((4, DIFF_DIM), fix), pl.BlockSpec((1, HEAD_DIM), fix)],
        out_specs=_out_spec(N_HEADS),
        out_shape=jax.ShapeDtypeStruct((batch, N_HEADS, seq, HEAD_DIM), BF16),
        compiler_params=_params("parallel", "arbitrary"),
        name="diff_attention",
    )(ht, ht, ht, lam_p, subln_g)


def _select_bias(score_t, keep_extra_t, ntop):
    nblk = score_t.shape[0]
    sub = lax.broadcasted_iota(jnp.int32, score_t.shape, 0)
    rank = jnp.zeros(score_t.shape, jnp.int32)
    for b in range(nblk):
        row = score_t[b:b + 1, :]
        ahead = (row > score_t) | ((row == score_t) & (sub > b))
        rank = rank + ahead.astype(jnp.int32)
    keep = rank < ntop
    if keep_extra_t is not None:
        keep = keep_extra_t(keep, sub)
    return jnp.where(keep, 0.0, NEG_INF)


def _bias_lanes(bias_t, lane0):
    nblk, t = bias_t.shape
    parts = []
    if lane0:
        parts.append(jnp.zeros((lane0, t), F32))
    parts.append(bias_t)
    if LANES - lane0 - nblk:
        parts.append(jnp.zeros((LANES - lane0 - nblk, t), F32))
    return jnp.concatenate(parts, axis=0).T


def _moba_gate_kernel(q_ref, k_ref, a_ref, o_ref, km_hi, km_lo, *, nblk):
    t = ATT_TILE
    i = pl.program_id(1)

    @pl.when(i == 0)
    def _():
        kmean = _dot(a_ref[...], k_ref[...])
        hi, lo = _split_bf16(kmean)
        km_hi[...] = hi
        km_lo[...] = lo

    q = q_ref[...]
    gate = _dot_nt(q, km_hi[...]) + _dot_nt(q, km_lo[...])
    gate_t = gate.T[HEAD_DIM:HEAD_DIM + nblk, :]
    blk = lax.broadcasted_iota(jnp.int32, gate_t.shape, 0)
    past = blk < i
    score_t = jnp.where(past, gate_t, NEG_INF)
    ksel = min(MOBA_TOPK, nblk - 1)

    def keep_rule(keep, sub):
        return (keep & (sub < i)) | (sub == i)

    bias_t = _select_bias(score_t, keep_rule, ksel)
    o_ref[...] = (q.astype(F32) + _bias_lanes(bias_t, HEAD_DIM)).astype(o_ref.dtype)


def _moba_gate(q_pad, k_pad, amat, batch, seq):
    t = ATT_TILE
    nblk = seq // MOBA_BLOCK
    nh = N_HEADS
    qs = pl.BlockSpec((None, None, t, LANES), lambda bh, i: (bh % nh, bh // nh, i, 0))
    ks = pl.BlockSpec((None, None, seq, LANES), lambda bh, i: (bh % nh, bh // nh, 0, 0))
    return pl.pallas_call(
        functools.partial(_moba_gate_kernel, nblk=nblk),
        grid=(batch * nh, seq // t),
        in_specs=[qs, ks, pl.BlockSpec((LANES, seq), lambda bh, i: (0, 0))],
        out_specs=qs,
        out_shape=jax.ShapeDtypeStruct((nh, batch, seq, LANES), BF16),
        scratch_shapes=[pltpu.VMEM((LANES, LANES), BF16), pltpu.VMEM((LANES, LANES), BF16)],
        compiler_params=_params("parallel", "arbitrary"),
        name="moba_gate",
    )(q_pad, k_pad, amat)


def _flash_kernel(q_ref, k_ref, v_ref, o_ref, *, window_tiles):
    t = ATT_TILE
    i = pl.program_id(1)
    q = q_ref[...]
    rows = q.shape[0]
    r = lax.broadcasted_iota(jnp.int32, (rows, t), 0) & (t - 1)
    c = lax.broadcasted_iota(jnp.int32, (rows, t), 1)

    def tile(j, st, mask):
        start = pl.multiple_of(j * t, t)
        k = k_ref[pl.ds(start, t), :]
        v = v_ref[pl.ds(start, t), :]
        s = _dot_nt(q, k)
        if mask is not None:
            s = jnp.where(mask, s, NEG_INF)
        return _online_softmax_step(s, v, *st)

    st = (jnp.full((rows, 1), NEG_INF, F32), jnp.zeros((rows, 1), F32),
          jnp.zeros((rows, v_ref.shape[-1]), F32))
    st = tile(i, st, c <= r)
    if window_tiles is None:
        st = lax.fori_loop(0, i, lambda j, s: tile(j, s, None), st)
    else:
        for back in range(1, window_tiles + 1):
            dead = jnp.where(i >= back, 0, t)
            j = jnp.maximum(i - back, 0)
            mask = (c > r + dead) if back == window_tiles else (c >= dead)
            st = tile(j, st, mask)
    _, l, acc = st
    o_ref[...] = (acc / l)[:, :HEAD_DIM].astype(o_ref.dtype)


def _flash(q, k, v, *, q_map, kv_map, o_map, grid, rows, seq, out_lead, out_dtype, window_tiles, name):
    vw = v.shape[-1]
    lead = (None,) * (q.ndim - 2)
    klead = (None,) * (k.ndim - 2)
    return pl.pallas_call(
        functools.partial(_flash_kernel, window_tiles=window_tiles),
        grid=grid,
        in_specs=[pl.BlockSpec(lead + (rows, LANES), q_map),
                  pl.BlockSpec(klead + (seq, LANES), kv_map),
                  pl.BlockSpec(klead + (seq, vw), kv_map)],
        out_specs=pl.BlockSpec(lead + (rows, HEAD_DIM), o_map),
        out_shape=jax.ShapeDtypeStruct(out_lead + (HEAD_DIM,), out_dtype),
        compiler_params=_params("parallel", "arbitrary"),
        name=name,
    )(q, k, v)


def _nsa_compress_kernel(tk_ref, tv_ref, pe_ref, wk_ref, wv_ref, g_ref, cos_ref, sa_ref, sb_ref,
                         kc_ref, vc_ref):
    rows = tk_ref.shape[0]

    def compress(tok, pe_lo, pe_hi, w_ref):
        half = w_ref.shape[0] // 2
        a = _dot((tok + pe_lo).astype(BF16), w_ref[0:half, :])
        b = _dot((tok + pe_hi).astype(BF16), w_ref[half:, :])
        return a + pltpu.roll(b, rows - 1, 0)

    kc = compress(tk_ref[...].astype(F32), pe_ref[0:1, :], pe_ref[1:2, :], wk_ref)
    vc = compress(tv_ref[...].astype(F32), pe_ref[2:3, :], pe_ref[3:4, :], wv_ref)
    ms = jnp.sum(kc * kc, axis=-1, keepdims=True) / HEAD_DIM
    kc = kc * lax.rsqrt(ms + RMS_EPS) * g_ref[...]
    kc = _rope_apply(kc, cos_ref[...], sa_ref[...], sb_ref[...], HEAD_DIM // ROPE_FRACTION_DEN // 2)
    kc_ref[...] = kc.astype(kc_ref.dtype)
    vc_ref[...] = vc.astype(vc_ref.dtype)


def _nsa_compress(tk, tv, pe4, wk, wv, gk, tabs, batch):
    rows, width = tk.shape[1], tk.shape[2]
    b3 = lambda b: (b, 0, 0)
    fix = lambda b: (0, 0)
    tok = pl.BlockSpec((None, rows, width), b3)
    out = pl.BlockSpec((None, rows, LANES), b3)
    tab = pl.BlockSpec((rows, LANES), fix)
    return pl.pallas_call(
        _nsa_compress_kernel,
        grid=(batch,),
        in_specs=[tok, tok, pl.BlockSpec((4, width), fix),
                  pl.BlockSpec((2 * width, LANES), fix), pl.BlockSpec((2 * width, LANES), fix),
                  pl.BlockSpec((1, LANES), fix), tab, tab, tab],
        out_specs=[out, out],
        out_shape=[jax.ShapeDtypeStruct((batch, rows, LANES), BF16)] * 2,
        compiler_params=_params("parallel"),
        name="nsa_compress",
    )(tk, tv, pe4, wk, wv, gk, *tabs)


def _nsa_select_kernel(q_ref, kc_ref, vc_ref, m_ref, qa_ref, oc_ref, *, nslc, ntop):
    t = ATT_TILE
    i = pl.program_id(1)
    q = q_ref[...]
    rows = q.shape[0]
    ncmp = kc_ref.shape[0]
    s = _dot_nt(q, kc_ref[...])
    qpos = i * t + (lax.broadcasted_iota(jnp.int32, (rows, ncmp), 0) & (t - 1))
    cmp_end = NSA_CMP_STRIDE * lax.broadcasted_iota(jnp.int32, (rows, ncmp), 1) + (NSA_CMP_LEN - 1)
    mask = cmp_end <= qpos
    s = jnp.where(mask, s, NEG_INF)
    e = jnp.where(mask, jnp.exp(s - jnp.max(s, axis=1, keepdims=True)), 0.0)
    p = e / jnp.maximum(jnp.sum(e, axis=1, keepdims=True), 1e-30)
    oc_ref[...] = _dot(p.astype(BF16), vc_ref[...])[:, :HEAD_DIM].astype(oc_ref.dtype)

    psum = p[0:t]
    for h in range(1, rows // t):
        psum = psum + p[h * t:(h + 1) * t]
    hi, lo = _split_bf16(psum)
    imp = _dot(hi, m_ref[...]) + _dot(lo, m_ref[...])
    imp_t = imp.T[HEAD_DIM:HEAD_DIM + nslc, :]
    blk = lax.broadcasted_iota(jnp.int32, imp_t.shape, 0)
    cur = (i * t + lax.broadcasted_iota(jnp.int32, imp_t.shape, 1)) // NSA_SLC_BLOCK
    forced = (blk == 0) | (blk == cur) | (blk == cur - 1)
    allowed = blk <= cur
    score_t = jnp.where(allowed, jnp.where(forced, BIG, imp_t), NEG_INF)
    bias = _bias_lanes(_select_bias(score_t, None, ntop), HEAD_DIM)
    qf = q.astype(F32)
    for h in range(rows // t):
        qa_ref[h * t:(h + 1) * t, :] = (qf[h * t:(h + 1) * t] + bias).astype(qa_ref.dtype)


def _nsa_select(q4, kc, vc, mmat, batch, seq):
    t = ATT_TILE
    rows = N_HEADS * t
    ncmp = kc.shape[1]
    nslc = seq // NSA_SLC_BLOCK
    qmap = lambda b, i: (b, i, 0, 0)
    kmap = lambda b, i: (b, 0, 0)
    return pl.pallas_call(
        functools.partial(_nsa_select_kernel, nslc=nslc, ntop=min(NSA_TOPN, nslc)),
        grid=(batch, seq // t),
        in_specs=[pl.BlockSpec((None, None, rows, LANES), qmap),
                  pl.BlockSpec((None, ncmp, LANES), kmap), pl.BlockSpec((None, ncmp, LANES), kmap),
                  pl.BlockSpec((ncmp, LANES), lambda b, i: (0, 0))],
        out_specs=[pl.BlockSpec((None, None, rows, LANES), qmap),
                   pl.BlockSpec((None, None, rows, HEAD_DIM), qmap)],
        out_shape=[jax.ShapeDtypeStruct((batch, seq // t, rows, LANES), BF16),
                   jax.ShapeDtypeStruct((batch, seq // t, rows, HEAD_DIM), F32)],
        compiler_params=_params("parallel", "arbitrary"),
        name="nsa_select",
    )(q4, kc, vc, mmat)


def _merge_kernel(x_ref, g_ref, oa_ref, ob_ref, oc_ref, dc_ref, ds_ref, dw_ref, gl_ref, e_ref,
                  wg_ref, bg_ref, wb_ref, wo_ref, y_ref):
    x = x_ref[...]
    xn = x * lax.rsqrt(jnp.mean(x * x, axis=-1, keepdims=True) + RMS_EPS) * g_ref[...]
    xn = xn.astype(BF16)
    sig = 1.0 / (1.0 + jnp.exp(-gl_ref[...].astype(F32)))
    hi, lo = _split_bf16(sig)
    gexp = _dot(hi, e_ref[...]) + _dot(lo, e_ref[...])
    w = MIX_WIDTH
    o_d = (gexp[:, 0:w] * dc_ref[...] + gexp[:, w:2 * w] * ds_ref[...]
           + gexp[:, 2 * w:3 * w] * dw_ref[...]).astype(BF16)
    merged = None
    for bi, o in enumerate((oa_ref[...], ob_ref[...], oc_ref[...], o_d)):
        gate = 1.0 / (1.0 + jnp.exp(-(_dot(xn, wg_ref[bi]) + bg_ref[bi])))
        term = gate * _dot(o, wb_ref[bi])
        merged = term if merged is None else merged + term
    y_ref[...] = x + _dot(merged.astype(BF16), wo_ref[...])


def _merge(x2, g, oa, ob, oc, dc, ds, dw, h, emat, wg, bg, wb, wo, tm):
    n, d = x2.shape
    row = lambda i: (i, 0)
    fix2 = lambda i: (0, 0)
    fix3 = lambda i: (0, 0, 0)
    o16 = pl.BlockSpec((tm, MIX_WIDTH), row)
    return pl.pallas_call(
        _merge_kernel,
        grid=(n // tm,),
        in_specs=[pl.BlockSpec((tm, d), row), _resident((1, d), fix2), o16, o16, o16, o16, o16, o16,
                  pl.BlockSpec((tm, GROUP), lambda i: (i, N_GROUPS - 1)),
                  _resident((GROUP, 3 * MIX_WIDTH), fix2),
                  _resident((4, d, d), fix3), _resident((4, 1, d), fix3),
                  _resident((4, MIX_WIDTH, d), fix3), _resident((d, d), fix2)],
        out_specs=pl.BlockSpec((tm, d), row),
        out_shape=jax.ShapeDtypeStruct((n, d), F32),
        compiler_params=_params("parallel"),
        name="gated_merge",
    )(x2, g, oa, ob, oc, dc, ds, dw, h, emat, wg, bg, wb, wo)


def _ffn_kernel(x_ref, g_ref, wg_ref, wu_ref, wd_ref, y_ref):
    x = x_ref[...]
    hn = x * lax.rsqrt(jnp.mean(x * x, axis=-1, keepdims=True) + RMS_EPS) * g_ref[...]
    hn = hn.astype(BF16)
    gate = _dot(hn, wg_ref[...])
    up = _dot(hn, wu_ref[...])
    act = gate * (1.0 / (1.0 + jnp.exp(-gate))) * up
    y_ref[...] = x + _dot(act.astype(BF16), wd_ref[...])


def _ffn(x2, g, wg, wu, wd, tm):
    n, d = x2.shape
    f = wg.shape[1]
    row = lambda i: (i, 0)
    fix = lambda i: (0, 0)
    return pl.pallas_call(
        _ffn_kernel,
        grid=(n // tm,),
        in_specs=[pl.BlockSpec((tm, d), row), _resident((1, d), fix), _resident((d, f), fix),
                  _resident((d, f), fix), _resident((f, d), fix)],
        out_specs=pl.BlockSpec((tm, d), row),
        out_shape=jax.ShapeDtypeStruct((n, d), F32),
        compiler_params=_params("parallel"),
        name="swiglu",
    )(x2, g, wg, wu, wd)


def _layer_constants(seq):
    t = ATT_TILE
    pos = jnp.arange(seq, dtype=jnp.int32)
    tabs64 = _rope_tables(pos, HEAD_DIM, GROUP)
    tabs32 = _rope_tables(pos, DIFF_DIM, GROUP)
    ncmp = seq // NSA_CMP_STRIDE
    cmp_end = NSA_CMP_STRIDE * jnp.arange(ncmp, dtype=jnp.int32) + (NSA_CMP_LEN - 1)
    tabs_cmp = _rope_tables(cmp_end, HEAD_DIM, HEAD_DIM)
    tabs_cmp = tuple(jnp.pad(a, ((0, 0), (0, LANES - HEAD_DIM)), constant_values=c)
                     for a, c in zip(tabs_cmp, (1.0, 0.0, 0.0)))

    s = np.arange(seq)
    nblk = seq // MOBA_BLOCK
    nslc = seq // NSA_SLC_BLOCK
    assert nblk <= LANES - HEAD_DIM and nslc <= LANES - HEAD_DIM and seq % t == 0
    moba_hot = np.zeros((seq, LANES - HEAD_DIM), np.float32)
    moba_hot[s, s // MOBA_BLOCK] = 1.0
    slc_hot = np.zeros((seq, LANES - HEAD_DIM), np.float32)
    slc_hot[s, s // NSA_SLC_BLOCK] = 1.0
    amat = np.zeros((LANES, seq), np.float32)
    amat[HEAD_DIM + s // MOBA_BLOCK, s] = 1.0 / MOBA_BLOCK

    nc_real = (seq - NSA_CMP_LEN) // NSA_CMP_STRIDE + 1
    cstart = NSA_CMP_STRIDE * np.arange(ncmp)
    sstart = NSA_SLC_BLOCK * np.arange(nslc)
    overlap = np.clip(np.minimum(cstart[:, None] + NSA_CMP_LEN, sstart[None, :] + NSA_SLC_BLOCK)
                      - np.maximum(cstart[:, None], sstart[None, :]), 0, None)
    mmat = np.zeros((ncmp, LANES), np.float32)
    mmat[:nc_real, HEAD_DIM:HEAD_DIM + nslc] = overlap[:nc_real].astype(np.float32) / NSA_CMP_STRIDE

    emat = np.zeros((GROUP, 3 * MIX_WIDTH), np.float32)
    for br in range(3):
        for hd in range(N_HEADS):
            emat[2 * HEAD_DIM + br * N_HEADS + hd,
                 br * MIX_WIDTH + hd * HEAD_DIM: br * MIX_WIDTH + (hd + 1) * HEAD_DIM] = 1.0

    return dict(tabs64=tabs64, tabs32=tabs32, tabs_cmp=tabs_cmp,
                moba_hot=jnp.asarray(moba_hot, BF16), slc_hot=jnp.asarray(slc_hot, BF16),
                amat=jnp.asarray(amat, BF16), mmat=jnp.asarray(mmat, BF16), emat=jnp.asarray(emat, BF16))


def _lane_vectors(diff_qn_g, diff_kn_g, moba_qn_g, moba_kn_g, nsa_qn_g, nsa_kn_g):
    ones = jnp.ones((GROUP,), F32)
    zeros = jnp.zeros((GROUP,), F32)
    sc64 = 1.0 / math.sqrt(HEAD_DIM)
    sc32 = 1.0 / math.sqrt(DIFF_DIM)
    quarter = lambda g, k: jnp.concatenate([g if j == k else jnp.ones((HEAD_DIM,), F32) for j in range(4)])
    flag = lambda k: jnp.concatenate([jnp.full((HEAD_DIM,), 1.0 if j == k else 0.0, F32) for j in range(4)])
    gains = [ones * sc64, ones, ones,
             jnp.tile(diff_qn_g, GROUP // DIFF_DIM) * sc32, jnp.tile(diff_kn_g, GROUP // DIFF_DIM), ones,
             jnp.tile(moba_qn_g, N_HEADS) * sc64, jnp.tile(moba_kn_g, N_HEADS), ones,
             jnp.tile(nsa_qn_g, N_HEADS) * sc64,
             quarter(nsa_kn_g[1], 2), quarter(nsa_kn_g[2], 0)]
    flags = [zeros, zeros, zeros, ones, ones, zeros, ones, ones, zeros, ones, flag(2), flag(0)]
    shape = (N_GROUPS, 1, GROUP)
    return jnp.stack(gains).reshape(shape), jnp.stack(flags).reshape(shape), jnp.stack(flags).reshape(shape)


def _row_tile(n, seq, want):
    tm = want if (n % want == 0 and seq % want == 0) else ATT_TILE
    return min(tm, seq)


def _mixers(x2, lw, consts, batch, seq, lam_init):
    n, d = x2.shape
    t = ATT_TILE
    nq = seq // t
    tm = _row_tile(n, seq, 512)

    gain, nflag, rflag = _lane_vectors(lw["diff_qn_g"], lw["diff_kn_g"], lw["moba_qn_g"],
                                       lw["moba_kn_g"], lw["nsa_qn_g"], lw["nsa_kn_g"])
    w_in = jnp.pad(lw["w_in"], ((0, 0), (0, N_GROUPS * GROUP - lw["w_in"].shape[1]))).astype(BF16)
    h = _inproj(x2, lw["attn_norm_g"].reshape(1, d), w_in, gain, nflag, rflag,
                consts["tabs64"], consts["tabs32"], seq, tm)

    ht = h.reshape(batch, seq, N_GROUPS * N_HEADS, HEAD_DIM).transpose(2, 0, 1, 3)
    pad64 = lambda a: jnp.pad(a, [(0, 0)] * (a.ndim - 1) + [(0, LANES - HEAD_DIM)])
    token_major = lambda o: o.transpose(0, 2, 1, 3).reshape(n, MIX_WIDTH)

    o_a = token_major(_stick_breaking(ht, batch, seq))
    o_b = token_major(_diff_attention(ht, lw["diff_lam"], lw["diff_subln_g"].reshape(1, HEAD_DIM),
                                      batch, seq, lam_init))

    mq_aug = _moba_gate(pad64(ht[24:28]), pad64(ht[28:32]), consts["amat"], batch, seq)
    hot = jnp.broadcast_to(consts["moba_hot"], (N_HEADS, batch, seq, LANES - HEAD_DIM))
    mk_aug = jnp.concatenate([ht[28:32], hot], axis=-1)
    bh_map = lambda bh, i: (bh % N_HEADS, bh // N_HEADS, i, 0)
    bh_full = lambda bh, i: (bh % N_HEADS, bh // N_HEADS, 0, 0)
    o_c = _flash(mq_aug, mk_aug, ht[32:36], q_map=bh_map, kv_map=bh_full,
                 o_map=lambda bh, i: (bh // N_HEADS, bh % N_HEADS, i, 0),
                 grid=(batch * N_HEADS, nq), rows=t, seq=seq, out_lead=(batch, N_HEADS, seq),
                 out_dtype=BF16, window_tiles=None, name="moba_attention")
    o_c = token_major(o_c)

    ncmp = seq // NSA_CMP_STRIDE
    tok = lambda a: a.reshape(batch, ncmp, NSA_CMP_STRIDE * HEAD_DIM)
    pe = lw["nsa_cmp_pe"]
    pe4 = pe.reshape(4, NSA_CMP_STRIDE * HEAD_DIM)
    wc = pad64(lw["nsa_cmp_w"]).astype(BF16)
    gk = pad64(lw["nsa_kn_g"][0].reshape(1, HEAD_DIM))
    kc, vc = _nsa_compress(tok(ht[40]), tok(ht[41]), pe4, wc[0], wc[1], gk, consts["tabs_cmp"], batch)

    nq4 = pad64(ht[36:40]).reshape(N_HEADS, batch, nq, t, LANES).transpose(1, 2, 0, 3, 4)
    nq4 = nq4.reshape(batch, nq, N_HEADS * t, LANES)
    q_aug, d_c = _nsa_select(nq4, kc, vc, consts["mmat"], batch, seq)

    rows = N_HEADS * t
    b_map = lambda b, i: (b, i, 0, 0)
    b_full = lambda b, i: (b, 0, 0)
    nsa_flash = functools.partial(_flash, q_map=b_map, kv_map=b_full, o_map=b_map, grid=(batch, nq),
                                  rows=rows, seq=seq, out_lead=(batch, nq, rows), out_dtype=F32)
    ks_aug = jnp.concatenate([ht[42], jnp.broadcast_to(consts["slc_hot"], (batch, seq, LANES - HEAD_DIM))],
                             axis=-1)
    d_s = nsa_flash(q_aug, ks_aug, ht[43], window_tiles=None, name="nsa_selected")
    d_w = nsa_flash(nq4, pad64(ht[44]), ht[45], window_tiles=NSA_WINDOW // t, name="nsa_window")
    stacked = lambda o: (o.reshape(batch, nq, N_HEADS, t, HEAD_DIM).transpose(0, 1, 3, 2, 4)
                         .reshape(n, MIX_WIDTH))
    return h, o_a, o_b, o_c, stacked(d_c), stacked(d_s), stacked(d_w)


def _layer(x2, lw, consts, batch, seq, lam_init):
    n, d = x2.shape
    h, o_a, o_b, o_c, d_c, d_s, d_w = _mixers(x2, lw, consts, batch, seq, lam_init)
    x2 = _merge(x2, lw["attn_norm_g"].reshape(1, d), o_a, o_b, o_c, d_c, d_s, d_w,
                h, consts["emat"], lw["w_gate"].astype(BF16), lw["b_gate"].reshape(4, 1, d),
                lw["w_branch"].astype(BF16), lw["w_out"].astype(BF16), _row_tile(n, seq, 512))
    return _ffn(x2, lw["ffn_norm_g"].reshape(1, d), lw["w_ffn_gate"].astype(BF16),
                lw["w_ffn_up"].astype(BF16), lw["w_ffn_down"].astype(BF16), _row_tile(n, seq, 256))


def kernel(x, attn_norm_g, w_in, diff_qn_g, diff_kn_g, diff_lam, diff_subln_g, moba_qn_g, moba_kn_g,
           nsa_qn_g, nsa_kn_g, nsa_cmp_pe, nsa_cmp_w, w_gate, b_gate, w_branch, w_out, ffn_norm_g,
           w_ffn_gate, w_ffn_up, w_ffn_down):
    batch, seq, d = x.shape
    weights = dict(attn_norm_g=attn_norm_g, w_in=w_in, diff_qn_g=diff_qn_g, diff_kn_g=diff_kn_g,
                   diff_lam=diff_lam, diff_subln_g=diff_subln_g, moba_qn_g=moba_qn_g, moba_kn_g=moba_kn_g,
                   nsa_qn_g=nsa_qn_g, nsa_kn_g=nsa_kn_g, nsa_cmp_pe=nsa_cmp_pe, nsa_cmp_w=nsa_cmp_w,
                   w_gate=w_gate, b_gate=b_gate, w_branch=w_branch, w_out=w_out, ffn_norm_g=ffn_norm_g,
                   w_ffn_gate=w_ffn_gate, w_ffn_up=w_ffn_up, w_ffn_down=w_ffn_down)
    consts = _layer_constants(seq)
    x2 = x.reshape(batch * seq, d)
    for layer in range(w_in.shape[0]):
        lw = {k: v[layer] for k, v in weights.items()}
        lam_init = 0.8 - 0.6 * math.exp(-0.3 * layer)
        x2 = _layer(x2, lw, consts, batch, seq, lam_init)
    return x2.reshape(batch, seq, d)
```

```python
import functools
import math

import numpy as np
import jax
import jax.numpy as jnp
from jax import lax
from jax.experimental import pallas as pl
from jax.experimental.pallas import tpu as pltpu

HEAD_DIM = 64
N_HEADS = 4
MIX_WIDTH = N_HEADS * HEAD_DIM
ROPE_THETA = 500000.0
ROPE_FRACTION_DEN = 4
DIFF_DIM = HEAD_DIM // 2
MOBA_BLOCK = 256
MOBA_TOPK = 3
NSA_CMP_LEN = 32
NSA_CMP_STRIDE = 16
NSA_SLC_BLOCK = 64
NSA_TOPN = 16
NSA_WINDOW = 512
NEG_INF = -1e30
BIG = 1e30
RMS_EPS = 1e-6

LANES = 128
ATT_TILE = 256
UNROLL = 4
GROUP = 256
N_GROUPS = 12
VMEM_LIMIT = 56 * 1024 * 1024
LOG2_E = math.log2(math.e)
EXP2_LIMIT = 80.0
NORM_SLACK = 1.05

F32 = jnp.float32
BF16 = jnp.bfloat16

_NT = (((1,), (1,)), ((), ()))


def _dot(a, b):
    return jnp.dot(a, b, preferred_element_type=F32)


def _dot_nt(a, b):
    return lax.dot_general(a, b, _NT, preferred_element_type=F32)


def _split_bf16(x):
    hi = x.astype(BF16)
    lo = (x - hi.astype(F32)).astype(BF16)
    return hi, lo


def _params(*sem):
    return pltpu.CompilerParams(dimension_semantics=sem, vmem_limit_bytes=VMEM_LIMIT)


def _resident(shape, index_map):
    return pl.BlockSpec(shape, index_map, pipeline_mode=pl.Buffered(1))


def _rope_tables(pos, d, width):
    r = d // ROPE_FRACTION_DEN
    half = r // 2
    inv = ROPE_THETA ** (-jnp.arange(half, dtype=F32) * 2.0 / r)
    ang = pos.astype(F32)[:, None] * inv[None, :]
    cos, sin = jnp.cos(ang), jnp.sin(ang)
    n = pos.shape[0]
    one = jnp.ones((n, d - r), F32)
    zero_h = jnp.zeros((n, half), F32)
    zero_t = jnp.zeros((n, d - r), F32)
    cos_g = jnp.concatenate([cos, cos, one], axis=1)
    sa_g = jnp.concatenate([-sin, zero_h, zero_t], axis=1)
    sb_g = jnp.concatenate([zero_h, sin, zero_t], axis=1)
    reps = width // d
    return (jnp.tile(cos_g, (1, reps)), jnp.tile(sa_g, (1, reps)), jnp.tile(sb_g, (1, reps)))


def _group_mean_matrix(d, width):
    g = np.arange(width) // d
    return jnp.asarray((g[:, None] == g[None, :]).astype(np.float32) / d, BF16)


def _rope_apply(y, cos, sa, sb, half):
    w = y.shape[-1]
    return y * cos + pltpu.roll(y, w - half, 1) * sa + pltpu.roll(y, half, 1) * sb


_GROUP_CFG = (
    (None, None), (None, None), (None, None),
    (DIFF_DIM, DIFF_DIM), (DIFF_DIM, DIFF_DIM), (None, None),
    (HEAD_DIM, HEAD_DIM), (HEAD_DIM, HEAD_DIM), (None, None),
    (HEAD_DIM, HEAD_DIM),
    (HEAD_DIM, HEAD_DIM),
    (HEAD_DIM, HEAD_DIM),
)


def _inproj_kernel(x_ref, g_ref, w_ref, gain_ref, nflag_ref, rflag_ref, b64_ref, b32_ref,
                   c64_ref, sa64_ref, sb64_ref, c32_ref, sa32_ref, sb32_ref, o_ref):
    x = x_ref[...]
    xn = x * lax.rsqrt(jnp.mean(x * x, axis=-1, keepdims=True) + RMS_EPS) * g_ref[...]
    xn = xn.astype(BF16)
    for gi, (norm, rope) in enumerate(_GROUP_CFG):
        cols = slice(gi * GROUP, (gi + 1) * GROUP)
        y = _dot(xn, w_ref[:, cols])
        gain = gain_ref[gi]
        if norm is not None:
            bmat = b64_ref[...] if norm == HEAD_DIM else b32_ref[...]
            ms = _dot((y * y).astype(BF16), bmat)
            y = y * jnp.where(nflag_ref[gi] > 0, lax.rsqrt(ms + RMS_EPS) * gain, gain)
        else:
            y = y * gain
        if rope is not None:
            if rope == HEAD_DIM:
                cos, sa, sb = c64_ref[...], sa64_ref[...], sb64_ref[...]
            else:
                cos, sa, sb = c32_ref[...], sa32_ref[...], sb32_ref[...]
            rf = rflag_ref[gi]
            cos = jnp.where(rf > 0, cos, 1.0)
            half = rope // ROPE_FRACTION_DEN // 2
            y = _rope_apply(y, cos, sa * rf, sb * rf, half)
        o_ref[:, cols] = y.astype(o_ref.dtype)


def _inproj(x2, g, w, gain, nflag, rflag, tabs64, tabs32, seq, tm):
    n, d = x2.shape
    ncols = N_GROUPS * GROUP
    nst = seq // tm
    row = lambda i: (i, 0)
    fix2 = lambda i: (0, 0)
    fix3 = lambda i: (0, 0, 0)
    tab = lambda i: (i % nst, 0)
    vec = _resident((N_GROUPS, 1, GROUP), fix3)
    tspec = pl.BlockSpec((tm, GROUP), tab)
    return pl.pallas_call(
        _inproj_kernel,
        grid=(n // tm,),
        in_specs=[pl.BlockSpec((tm, d), row), _resident((1, d), fix2), _resident((d, ncols), fix2),
                  vec, vec, vec, _resident((GROUP, GROUP), fix2), _resident((GROUP, GROUP), fix2),
                  tspec, tspec, tspec, tspec, tspec, tspec],
        out_specs=pl.BlockSpec((tm, ncols), row),
        out_shape=jax.ShapeDtypeStruct((n, ncols), BF16),
        compiler_params=_params("parallel"),
        name="inproj",
    )(x2, g, w, gain, nflag, rflag, _group_mean_matrix(HEAD_DIM, GROUP),
      _group_mean_matrix(DIFF_DIM, GROUP), *tabs64, *tabs32)


def _col_minus_row(nrows, tq=None):
    shape = (nrows, ATT_TILE)
    row = lax.broadcasted_iota(jnp.int32, shape, 0)
    if tq is not None:
        row = row & (tq - 1)
    return lax.broadcasted_iota(jnp.int32, shape, 1) - row


def _causal_sweep(step, i, rows, tq, strict):
    t = ATT_TILE
    per_block = tq // t
    base = i * per_block
    tile = lambda j: pl.multiple_of(j * t, t)
    visible = (lambda cmr: cmr < 0) if strict else (lambda cmr: cmr <= 0)
    if tq == t:
        step(slice(0, rows), [tile(base)], visible(_col_minus_row(rows, tq)))
    else:
        assert rows == tq
        own = visible(_col_minus_row(t))
        for d in range(per_block - 1, -1, -1):
            step(slice(d * t, (d + 1) * t), [tile(base + d)], own)
            if (d + 1) * t < rows:
                step(slice((d + 1) * t, rows), [tile(base + d)], None)

    groups = base // UNROLL

    def body(g, carry):
        newest = base - 1 - g * UNROLL
        step(slice(0, rows), [tile(newest - u) for u in range(UNROLL)], None)
        return carry

    lax.fori_loop(0, groups, body, 0)
    left = base - groups * UNROLL
    for r in range(UNROLL - 1, 0, -1):
        @pl.when(left >= r)
        def _():
            step(slice(0, rows), [tile(r - 1)], None)


def _sb_kernel(q_ref, k_ref, v_ref, o_ref, carry_ref, acc_ref, *, tq):
    t = ATT_TILE
    i = pl.program_id(1)
    rows = q_ref.shape[0]
    r = lax.broadcasted_iota(jnp.int32, (t, t), 0)
    c = lax.broadcasted_iota(jnp.int32, (t, t), 1)
    suffix = (r > c).astype(BF16)

    def step(rs, starts, mask):
        n = rs.stop - rs.start
        q = q_ref[rs, :]
        carry = carry_ref[rs, :]
        pv = None
        for start in starts:
            z = _dot_nt(q, k_ref[pl.ds(start, t), :])
            nz = -z
            l1m = jnp.minimum(nz, 0.0) - jnp.log2(1.0 + jnp.exp2(jnp.minimum(z, nz)))
            logsig = z + l1m
            if mask is not None:
                l1m = jnp.where(mask, l1m, 0.0)
            hi, lo = _split_bf16(l1m)
            both = _dot(jnp.concatenate([hi, lo], axis=0), suffix)
            local = both[:n] + both[n:]
            a = jnp.exp2(logsig + local + carry)
            if mask is not None:
                a = jnp.where(mask, a, 0.0)
            d = _dot(a.astype(BF16), v_ref[pl.ds(start, t), :])
            pv = d if pv is None else pv + d
            carry = carry + (local[:, 0:1] + l1m[:, 0:1])
        acc_ref[rs, :] += pv
        carry_ref[rs, :] = carry

    carry_ref[...] = jnp.zeros_like(carry_ref)
    acc_ref[...] = jnp.zeros_like(acc_ref)
    _causal_sweep(step, i, rows, tq, True)
    o_ref[...] = acc_ref[...].astype(o_ref.dtype)


def _q_block(seq):
    return min(4 * ATT_TILE, seq)


def _head_specs(base, nheads, seq, width, tq):
    qs = pl.BlockSpec((None, None, tq, width), lambda bh, i: (base + bh % nheads, bh // nheads, i, 0))
    full = lambda b0, w=width: pl.BlockSpec((None, None, seq, w),
                                            lambda bh, i: (b0 + bh % nheads, bh // nheads, 0, 0))
    return qs, full


def _out_spec(nheads, tq):
    return pl.BlockSpec((None, None, tq, HEAD_DIM), lambda bh, i: (bh // nheads, bh % nheads, i, 0))


def _stick_breaking(ht, batch, seq):
    tq = _q_block(seq)
    qs, full = _head_specs(0, N_HEADS, seq, HEAD_DIM, tq)
    return pl.pallas_call(
        functools.partial(_sb_kernel, tq=tq),
        grid=(batch * N_HEADS, seq // tq),
        in_specs=[qs, full(4), full(8)],
        out_specs=_out_spec(N_HEADS, tq),
        out_shape=jax.ShapeDtypeStruct((batch, N_HEADS, seq, HEAD_DIM), BF16),
        scratch_shapes=[pltpu.VMEM((tq, 1), F32), pltpu.VMEM((tq, HEAD_DIM), F32)],
        compiler_params=_params("parallel", "arbitrary"),
        name="stick_breaking",
    )(ht, ht, ht)


def _softmax_step(q, k_ref, v_ref, starts, mask, rs, m_ref, acc_ref):
    t = ATT_TILE
    pv = None
    for start in starts:
        s = _dot_nt(q, k_ref[pl.ds(start, t), :])
        if mask is not None:
            s = jnp.where(mask, s, NEG_INF)
        v1 = v_ref[pl.ds(start, t), :]
        if m_ref is None:
            d = _dot(jnp.exp2(s).astype(BF16), v1)
            pv = d if pv is None else pv + d
            continue
        m_old = m_ref[rs, :]
        mn = jnp.maximum(m_old, jnp.max(s, axis=1, keepdims=True))
        acc_ref[rs, :] = jnp.exp2(m_old - mn) * acc_ref[rs, :] + _dot(jnp.exp2(s - mn).astype(BF16), v1)
        m_ref[rs, :] = mn
    if m_ref is None:
        acc_ref[rs, :] += pv


def _either_path(bound_ref, m_refs, sweep):
    bounded = bound_ref[0] <= EXP2_LIMIT

    @pl.when(bounded)
    def _():
        sweep((None,) * len(m_refs))

    @pl.when(jnp.logical_not(bounded))
    def _():
        for ref in m_refs:
            ref[...] = jnp.full(ref.shape, NEG_INF, F32)
        sweep(m_refs)


def _normalised(acc):
    return acc[:, :HEAD_DIM] / acc[:, HEAD_DIM:HEAD_DIM + 1]


def _diff_kernel(bound_ref, q_ref, k_ref, v_ref, lam_ref, g_ref, o_ref, q1_ref, q2_ref, m1_ref, m2_ref,
                 a1_ref, a2_ref, *, lam_init, tq):
    t = ATT_TILE
    i = pl.program_id(1)
    rows = q_ref.shape[0]
    q = q_ref[...]
    lane = lax.broadcasted_iota(jnp.int32, q.shape, 1)
    zero = jnp.zeros_like(q)
    q1_ref[...] = jnp.where(lane < DIFF_DIM, q, zero)
    q2_ref[...] = jnp.where(lane >= DIFF_DIM, q, zero)
    for ref in (a1_ref, a2_ref):
        ref[...] = jnp.zeros_like(ref)

    def sweep(m_refs):
        def step(rs, starts, mask):
            _softmax_step(q1_ref[rs, :], k_ref, v_ref, starts, mask, rs, m_refs[0], a1_ref)
            _softmax_step(q2_ref[rs, :], k_ref, v_ref, starts, mask, rs, m_refs[1], a2_ref)

        _causal_sweep(step, i, rows, tq, False)

    _either_path(bound_ref, (m1_ref, m2_ref), sweep)

    lp = lam_ref[...]
    lam = (jnp.exp(jnp.sum(lp[0:1, :] * lp[1:2, :], axis=1, keepdims=True))
           - jnp.exp(jnp.sum(lp[2:3, :] * lp[3:4, :], axis=1, keepdims=True)) + lam_init)
    o = _normalised(a1_ref[...]) - lam * _normalised(a2_ref[...])
    o = o * lax.rsqrt(jnp.mean(o * o, axis=-1, keepdims=True) + RMS_EPS) * g_ref[...]
    o_ref[...] = (o * (1.0 - lam_init)).astype(o_ref.dtype)


_SCALAR = pl.BlockSpec(memory_space=pltpu.SMEM)


def _diff_attention(bound, ht, v1, lam_p, subln_g, batch, seq, lam_init):
    tq = _q_block(seq)
    qs, full = _head_specs(12, N_HEADS, seq, HEAD_DIM, tq)
    _, full_v = _head_specs(0, N_HEADS, seq, LANES, tq)
    fix = lambda bh, i: (0, 0)
    return pl.pallas_call(
        functools.partial(_diff_kernel, lam_init=lam_init, tq=tq),
        grid=(batch * N_HEADS, seq // tq),
        in_specs=[_SCALAR, qs, full(16), full_v(0),
                  pl.BlockSpec((4, DIFF_DIM), fix), pl.BlockSpec((1, HEAD_DIM), fix)],
        out_specs=_out_spec(N_HEADS, tq),
        out_shape=jax.ShapeDtypeStruct((batch, N_HEADS, seq, HEAD_DIM), BF16),
        scratch_shapes=[pltpu.VMEM((tq, HEAD_DIM), BF16), pltpu.VMEM((tq, HEAD_DIM), BF16),
                        pltpu.VMEM((tq, 1), F32), pltpu.VMEM((tq, 1), F32),
                        pltpu.VMEM((tq, LANES), F32), pltpu.VMEM((tq, LANES), F32)],
        compiler_params=_params("parallel", "arbitrary"),
        name="diff_attention",
    )(bound, ht, ht, v1, lam_p, subln_g)


def _select_bias(score_t, keep_extra_t, ntop):
    nblk = score_t.shape[0]
    sub = lax.broadcasted_iota(jnp.int32, score_t.shape, 0)
    rank = jnp.zeros(score_t.shape, jnp.int32)
    for b in range(nblk):
        row = score_t[b:b + 1, :]
        ahead = (row > score_t) | ((row == score_t) & (sub > b))
        rank = rank + ahead.astype(jnp.int32)
    keep = rank < ntop
    if keep_extra_t is not None:
        keep = keep_extra_t(keep, sub)
    return jnp.where(keep, 0.0, NEG_INF)


def _bias_lanes(bias_t, lane0):
    nblk, t = bias_t.shape
    parts = []
    if lane0:
        parts.append(jnp.zeros((lane0, t), F32))
    parts.append(bias_t)
    if LANES - lane0 - nblk:
        parts.append(jnp.zeros((LANES - lane0 - nblk, t), F32))
    return jnp.concatenate(parts, axis=0).T


def _moba_gate_kernel(q_ref, k_ref, a_ref, o_ref, km_hi, km_lo, *, nblk):
    t = ATT_TILE
    i = pl.program_id(1)

    @pl.when(i == 0)
    def _():
        kmean = _dot(a_ref[...], k_ref[...])
        hi, lo = _split_bf16(kmean)
        km_hi[...] = hi
        km_lo[...] = lo

    q = q_ref[...]
    gate = _dot_nt(q, km_hi[...]) + _dot_nt(q, km_lo[...])
    gate_t = gate.T[HEAD_DIM:HEAD_DIM + nblk, :]
    blk = lax.broadcasted_iota(jnp.int32, gate_t.shape, 0)
    past = blk < i
    score_t = jnp.where(past, gate_t, NEG_INF)
    ksel = min(MOBA_TOPK, nblk - 1)

    def keep_rule(keep, sub):
        return (keep & (sub < i)) | (sub == i)

    bias_t = _select_bias(score_t, keep_rule, ksel)
    o_ref[...] = (q.astype(F32) + _bias_lanes(bias_t, HEAD_DIM)).astype(o_ref.dtype)


def _moba_gate(q_pad, k_pad, amat, batch, seq):
    t = ATT_TILE
    nblk = seq // MOBA_BLOCK
    nh = N_HEADS
    qs = pl.BlockSpec((None, None, t, LANES), lambda bh, i: (bh % nh, bh // nh, i, 0))
    ks = pl.BlockSpec((None, None, seq, LANES), lambda bh, i: (bh % nh, bh // nh, 0, 0))
    return pl.pallas_call(
        functools.partial(_moba_gate_kernel, nblk=nblk),
        grid=(batch * nh, seq // t),
        in_specs=[qs, ks, pl.BlockSpec((LANES, seq), lambda bh, i: (0, 0))],
        out_specs=qs,
        out_shape=jax.ShapeDtypeStruct((nh, batch, seq, LANES), BF16),
        scratch_shapes=[pltpu.VMEM((LANES, LANES), BF16), pltpu.VMEM((LANES, LANES), BF16)],
        compiler_params=_params("parallel", "arbitrary"),
        name="moba_gate",
    )(q_pad, k_pad, amat)


def _flash_kernel(bound_ref, q_ref, k_ref, v_ref, o_ref, m_ref, acc_ref, *, tq, window_tiles):
    t = ATT_TILE
    i = pl.program_id(1)
    rows = q_ref.shape[0]
    acc_ref[...] = jnp.zeros_like(acc_ref)

    def sweep(m_refs):
        def step(rs, starts, mask):
            _softmax_step(q_ref[rs, :], k_ref, v_ref, starts, mask, rs, m_refs[0], acc_ref)

        if window_tiles is None:
            _causal_sweep(step, i, rows, tq, False)
            return
        assert tq == t
        step(slice(0, rows), [pl.multiple_of(i * t, t)], _col_minus_row(rows, tq) <= 0)
        for back in range(1, window_tiles + 1):
            @pl.when(i >= back)
            def _():
                mask = (_col_minus_row(rows, tq) > 0) if back == window_tiles else None
                step(slice(0, rows), [pl.multiple_of((i - back) * t, t)], mask)

    _either_path(bound_ref, (m_ref,), sweep)
    o_ref[...] = _normalised(acc_ref[...]).astype(o_ref.dtype)


def _flash(bound, q, k, v1, *, q_map, kv_map, o_map, grid, rows, tq, seq, out_lead, out_dtype, window_tiles,
           name):
    lead = (None,) * (q.ndim - 2)
    klead = (None,) * (k.ndim - 2)
    return pl.pallas_call(
        functools.partial(_flash_kernel, tq=tq, window_tiles=window_tiles),
        grid=grid,
        in_specs=[_SCALAR, pl.BlockSpec(lead + (rows, LANES), q_map),
                  pl.BlockSpec(klead + (seq, LANES), kv_map),
                  pl.BlockSpec(klead + (seq, LANES), kv_map)],
        out_specs=pl.BlockSpec(lead + (rows, HEAD_DIM), o_map),
        out_shape=jax.ShapeDtypeStruct(out_lead + (HEAD_DIM,), out_dtype),
        scratch_shapes=[pltpu.VMEM((rows, 1), F32), pltpu.VMEM((rows, LANES), F32)],
        compiler_params=_params("parallel", "arbitrary"),
        name=name,
    )(bound, q, k, v1)


def _nsa_compress_kernel(tk_ref, tv_ref, pe_ref, wk_ref, wv_ref, g_ref, cos_ref, sa_ref, sb_ref,
                         kc_ref, vc_ref):
    rows = tk_ref.shape[0]

    def compress(tok, pe_lo, pe_hi, w_ref):
        half = w_ref.shape[0] // 2
        a = _dot((tok + pe_lo).astype(BF16), w_ref[0:half, :])
        b = _dot((tok + pe_hi).astype(BF16), w_ref[half:, :])
        return a + pltpu.roll(b, rows - 1, 0)

    kc = compress(tk_ref[...].astype(F32), pe_ref[0:1, :], pe_ref[1:2, :], wk_ref)
    vc = compress(tv_ref[...].astype(F32), pe_ref[2:3, :], pe_ref[3:4, :], wv_ref)
    ms = jnp.sum(kc * kc, axis=-1, keepdims=True) / HEAD_DIM
    kc = kc * lax.rsqrt(ms + RMS_EPS) * g_ref[...]
    kc = _rope_apply(kc, cos_ref[...], sa_ref[...], sb_ref[...], HEAD_DIM // ROPE_FRACTION_DEN // 2)
    kc_ref[...] = kc.astype(kc_ref.dtype)
    vc_ref[...] = vc.astype(vc_ref.dtype)


def _nsa_compress(tk, tv, pe4, wk, wv, gk, tabs, batch):
    rows, width = tk.shape[1], tk.shape[2]
    b3 = lambda b: (b, 0, 0)
    fix = lambda b: (0, 0)
    tok = pl.BlockSpec((None, rows, width), b3)
    out = pl.BlockSpec((None, rows, LANES), b3)
    tab = pl.BlockSpec((rows, LANES), fix)
    return pl.pallas_call(
        _nsa_compress_kernel,
        grid=(batch,),
        in_specs=[tok, tok, pl.BlockSpec((4, width), fix),
                  pl.BlockSpec((2 * width, LANES), fix), pl.BlockSpec((2 * width, LANES), fix),
                  pl.BlockSpec((1, LANES), fix), tab, tab, tab],
        out_specs=[out, out],
        out_shape=[jax.ShapeDtypeStruct((batch, rows, LANES), BF16)] * 2,
        compiler_params=_params("parallel"),
        name="nsa_compress",
    )(tk, tv, pe4, wk, wv, gk, *tabs)


def _nsa_select_kernel(q_ref, kc_ref, vc_ref, m_ref, qa_ref, oc_ref, *, nslc, ntop):
    t = ATT_TILE
    i = pl.program_id(1)
    q = q_ref[...]
    rows = q.shape[0]
    ncmp = kc_ref.shape[0]
    s = _dot_nt(q, kc_ref[...])
    qpos = i * t + (lax.broadcasted_iota(jnp.int32, (rows, ncmp), 0) & (t - 1))
    cmp_end = NSA_CMP_STRIDE * lax.broadcasted_iota(jnp.int32, (rows, ncmp), 1) + (NSA_CMP_LEN - 1)
    mask = cmp_end <= qpos
    s = jnp.where(mask, s, NEG_INF)
    e = jnp.where(mask, jnp.exp2(s - jnp.max(s, axis=1, keepdims=True)), 0.0)
    p = e / jnp.maximum(jnp.sum(e, axis=1, keepdims=True), 1e-30)
    oc_ref[...] = _dot(p.astype(BF16), vc_ref[...])[:, :HEAD_DIM].astype(oc_ref.dtype)

    psum = p[0:t]
    for h in range(1, rows // t):
        psum = psum + p[h * t:(h + 1) * t]
    hi, lo = _split_bf16(psum)
    imp = _dot(hi, m_ref[...]) + _dot(lo, m_ref[...])
    imp_t = imp.T[HEAD_DIM:HEAD_DIM + nslc, :]
    blk = lax.broadcasted_iota(jnp.int32, imp_t.shape, 0)
    cur = (i * t + lax.broadcasted_iota(jnp.int32, imp_t.shape, 1)) // NSA_SLC_BLOCK
    forced = (blk == 0) | (blk == cur) | (blk == cur - 1)
    allowed = blk <= cur
    score_t = jnp.where(allowed, jnp.where(forced, BIG, imp_t), NEG_INF)
    bias = _bias_lanes(_select_bias(score_t, None, ntop), HEAD_DIM)
    qf = q.astype(F32)
    for h in range(rows // t):
        qa_ref[h * t:(h + 1) * t, :] = (qf[h * t:(h + 1) * t] + bias).astype(qa_ref.dtype)


def _nsa_select(q4, kc, vc, mmat, batch, seq):
    t = ATT_TILE
    rows = N_HEADS * t
    ncmp = kc.shape[1]
    nslc = seq // NSA_SLC_BLOCK
    qmap = lambda b, i: (b, i, 0, 0)
    kmap = lambda b, i: (b, 0, 0)
    return pl.pallas_call(
        functools.partial(_nsa_select_kernel, nslc=nslc, ntop=min(NSA_TOPN, nslc)),
        grid=(batch, seq // t),
        in_specs=[pl.BlockSpec((None, None, rows, LANES), qmap),
                  pl.BlockSpec((None, ncmp, LANES), kmap), pl.BlockSpec((None, ncmp, LANES), kmap),
                  pl.BlockSpec((ncmp, LANES), lambda b, i: (0, 0))],
        out_specs=[pl.BlockSpec((None, None, rows, LANES), qmap),
                   pl.BlockSpec((None, None, rows, HEAD_DIM), qmap)],
        out_shape=[jax.ShapeDtypeStruct((batch, seq // t, rows, LANES), BF16),
                   jax.ShapeDtypeStruct((batch, seq // t, rows, HEAD_DIM), F32)],
        compiler_params=_params("parallel", "arbitrary"),
        name="nsa_select",
    )(q4, kc, vc, mmat)


def _merge_kernel(x_ref, g_ref, oa_ref, ob_ref, oc_ref, dc_ref, ds_ref, dw_ref, gl_ref, e_ref,
                  wg_ref, bg_ref, wb_ref, wo_ref, y_ref):
    x = x_ref[...]
    xn = x * lax.rsqrt(jnp.mean(x * x, axis=-1, keepdims=True) + RMS_EPS) * g_ref[...]
    xn = xn.astype(BF16)
    sig = 1.0 / (1.0 + jnp.exp(-gl_ref[...].astype(F32)))
    hi, lo = _split_bf16(sig)
    gexp = _dot(hi, e_ref[...]) + _dot(lo, e_ref[...])
    w = MIX_WIDTH
    o_d = (gexp[:, 0:w] * dc_ref[...] + gexp[:, w:2 * w] * ds_ref[...]
           + gexp[:, 2 * w:3 * w] * dw_ref[...]).astype(BF16)
    merged = None
    for bi, o in enumerate((oa_ref[...], ob_ref[...], oc_ref[...], o_d)):
        gate = 1.0 / (1.0 + jnp.exp(-(_dot(xn, wg_ref[bi]) + bg_ref[bi])))
        term = gate * _dot(o, wb_ref[bi])
        merged = term if merged is None else merged + term
    y_ref[...] = x + _dot(merged.astype(BF16), wo_ref[...])


def _merge(x2, g, oa, ob, oc, dc, ds, dw, h, emat, wg, bg, wb, wo, tm):
    n, d = x2.shape
    row = lambda i: (i, 0)
    fix2 = lambda i: (0, 0)
    fix3 = lambda i: (0, 0, 0)
    o16 = pl.BlockSpec((tm, MIX_WIDTH), row)
    return pl.pallas_call(
        _merge_kernel,
        grid=(n // tm,),
        in_specs=[pl.BlockSpec((tm, d), row), _resident((1, d), fix2), o16, o16, o16, o16, o16, o16,
                  pl.BlockSpec((tm, GROUP), lambda i: (i, N_GROUPS - 1)),
                  _resident((GROUP, 3 * MIX_WIDTH), fix2),
                  _resident((4, d, d), fix3), _resident((4, 1, d), fix3),
                  _resident((4, MIX_WIDTH, d), fix3), _resident((d, d), fix2)],
        out_specs=pl.BlockSpec((tm, d), row),
        out_shape=jax.ShapeDtypeStruct((n, d), F32),
        compiler_params=_params("parallel"),
        name="gated_merge",
    )(x2, g, oa, ob, oc, dc, ds, dw, h, emat, wg, bg, wb, wo)


def _ffn_kernel(x_ref, g_ref, wg_ref, wu_ref, wd_ref, y_ref):
    x = x_ref[...]
    hn = x * lax.rsqrt(jnp.mean(x * x, axis=-1, keepdims=True) + RMS_EPS) * g_ref[...]
    hn = hn.astype(BF16)
    gate = _dot(hn, wg_ref[...])
    up = _dot(hn, wu_ref[...])
    act = gate * (1.0 / (1.0 + jnp.exp(-gate))) * up
    y_ref[...] = x + _dot(act.astype(BF16), wd_ref[...])


def _ffn(x2, g, wg, wu, wd, tm):
    n, d = x2.shape
    f = wg.shape[1]
    row = lambda i: (i, 0)
    fix = lambda i: (0, 0)
    return pl.pallas_call(
        _ffn_kernel,
        grid=(n // tm,),
        in_specs=[pl.BlockSpec((tm, d), row), _resident((1, d), fix), _resident((d, f), fix),
                  _resident((d, f), fix), _resident((f, d), fix)],
        out_specs=pl.BlockSpec((tm, d), row),
        out_shape=jax.ShapeDtypeStruct((n, d), F32),
        compiler_params=_params("parallel"),
        name="swiglu",
    )(x2, g, wg, wu, wd)


def _layer_constants(seq):
    t = ATT_TILE
    pos = jnp.arange(seq, dtype=jnp.int32)
    tabs64 = _rope_tables(pos, HEAD_DIM, GROUP)
    tabs32 = _rope_tables(pos, DIFF_DIM, GROUP)
    ncmp = seq // NSA_CMP_STRIDE
    cmp_end = NSA_CMP_STRIDE * jnp.arange(ncmp, dtype=jnp.int32) + (NSA_CMP_LEN - 1)
    tabs_cmp = _rope_tables(cmp_end, HEAD_DIM, HEAD_DIM)
    tabs_cmp = tuple(jnp.pad(a, ((0, 0), (0, LANES - HEAD_DIM)), constant_values=c)
                     for a, c in zip(tabs_cmp, (1.0, 0.0, 0.0)))

    s = np.arange(seq)
    nblk = seq // MOBA_BLOCK
    nslc = seq // NSA_SLC_BLOCK
    assert nblk <= LANES - HEAD_DIM and nslc <= LANES - HEAD_DIM and seq % t == 0
    moba_hot = np.zeros((seq, LANES - HEAD_DIM), np.float32)
    moba_hot[s, s // MOBA_BLOCK] = 1.0
    slc_hot = np.zeros((seq, LANES - HEAD_DIM), np.float32)
    slc_hot[s, s // NSA_SLC_BLOCK] = 1.0
    amat = np.zeros((LANES, seq), np.float32)
    amat[HEAD_DIM + s // MOBA_BLOCK, s] = 1.0 / MOBA_BLOCK

    nc_real = (seq - NSA_CMP_LEN) // NSA_CMP_STRIDE + 1
    cstart = NSA_CMP_STRIDE * np.arange(ncmp)
    sstart = NSA_SLC_BLOCK * np.arange(nslc)
    overlap = np.clip(np.minimum(cstart[:, None] + NSA_CMP_LEN, sstart[None, :] + NSA_SLC_BLOCK)
                      - np.maximum(cstart[:, None], sstart[None, :]), 0, None)
    mmat = np.zeros((ncmp, LANES), np.float32)
    mmat[:nc_real, HEAD_DIM:HEAD_DIM + nslc] = overlap[:nc_real].astype(np.float32) / NSA_CMP_STRIDE

    emat = np.zeros((GROUP, 3 * MIX_WIDTH), np.float32)
    for br in range(3):
        for hd in range(N_HEADS):
            emat[2 * HEAD_DIM + br * N_HEADS + hd,
                 br * MIX_WIDTH + hd * HEAD_DIM: br * MIX_WIDTH + (hd + 1) * HEAD_DIM] = 1.0

    return dict(tabs64=tabs64, tabs32=tabs32, tabs_cmp=tabs_cmp,
                moba_hot=jnp.asarray(moba_hot, BF16), slc_hot=jnp.asarray(slc_hot, BF16),
                amat=jnp.asarray(amat, BF16), mmat=jnp.asarray(mmat, BF16), emat=jnp.asarray(emat, BF16))


def _lane_vectors(diff_qn_g, diff_kn_g, moba_qn_g, moba_kn_g, nsa_qn_g, nsa_kn_g):
    ones = jnp.ones((GROUP,), F32)
    zeros = jnp.zeros((GROUP,), F32)
    sc64_log2 = LOG2_E / math.sqrt(HEAD_DIM)
    sc32_log2 = LOG2_E / math.sqrt(DIFF_DIM)
    quarter = lambda g, k: jnp.concatenate([g if j == k else jnp.ones((HEAD_DIM,), F32) for j in range(4)])
    flag = lambda k: jnp.concatenate([jnp.full((HEAD_DIM,), 1.0 if j == k else 0.0, F32) for j in range(4)])
    gains = [ones * sc64_log2, ones, ones,
             jnp.tile(diff_qn_g, GROUP // DIFF_DIM) * sc32_log2, jnp.tile(diff_kn_g, GROUP // DIFF_DIM), ones,
             jnp.tile(moba_qn_g, N_HEADS) * sc64_log2, jnp.tile(moba_kn_g, N_HEADS), ones,
             jnp.tile(nsa_qn_g, N_HEADS) * sc64_log2,
             quarter(nsa_kn_g[1], 2), quarter(nsa_kn_g[2], 0)]
    flags = [zeros, zeros, zeros, ones, ones, zeros, ones, ones, zeros, ones, flag(2), flag(0)]
    shape = (N_GROUPS, 1, GROUP)
    return jnp.stack(gains).reshape(shape), jnp.stack(flags).reshape(shape), jnp.stack(flags).reshape(shape)


def _row_tile(n, seq, want):
    tm = want if (n % want == 0 and seq % want == 0) else ATT_TILE
    return min(tm, seq)


def _mixers(x2, lw, consts, batch, seq, lam_init):
    n, d = x2.shape
    t = ATT_TILE
    nq = seq // t
    tm = _row_tile(n, seq, 512)

    gain, nflag, rflag = _lane_vectors(lw["diff_qn_g"], lw["diff_kn_g"], lw["moba_qn_g"],
                                       lw["moba_kn_g"], lw["nsa_qn_g"], lw["nsa_kn_g"])
    w_in = jnp.pad(lw["w_in"], ((0, 0), (0, N_GROUPS * GROUP - lw["w_in"].shape[1]))).astype(BF16)
    h = _inproj(x2, lw["attn_norm_g"].reshape(1, d), w_in, gain, nflag, rflag,
                consts["tabs64"], consts["tabs32"], seq, tm)

    ht = h.reshape(batch, seq, N_GROUPS * N_HEADS, HEAD_DIM).transpose(2, 0, 1, 3)
    pad64 = lambda a: jnp.pad(a, [(0, 0)] * (a.ndim - 1) + [(0, LANES - HEAD_DIM)])
    token_major = lambda o: o.transpose(0, 2, 1, 3).reshape(n, MIX_WIDTH)
    one_lane = jnp.zeros((LANES - HEAD_DIM,), BF16).at[0].set(1.0)
    with_ones = lambda v: jnp.concatenate(
        [v, jnp.broadcast_to(one_lane, v.shape[:-1] + (LANES - HEAD_DIM,))], axis=-1)

    amax = lambda g: jnp.max(jnp.abs(g))
    score_bound = lambda d_head, gq, gk: (math.sqrt(d_head) * LOG2_E * NORM_SLACK * NORM_SLACK
                                          * amax(gq) * amax(gk)).reshape(1).astype(F32)

    o_a = token_major(_stick_breaking(ht, batch, seq))
    o_b = token_major(_diff_attention(score_bound(DIFF_DIM, lw["diff_qn_g"], lw["diff_kn_g"]), ht,
                                      with_ones(ht[20:24]), lw["diff_lam"],
                                      lw["diff_subln_g"].reshape(1, HEAD_DIM), batch, seq, lam_init))

    mq_aug = _moba_gate(pad64(ht[24:28]), pad64(ht[28:32]), consts["amat"], batch, seq)
    hot = jnp.broadcast_to(consts["moba_hot"], (N_HEADS, batch, seq, LANES - HEAD_DIM))
    mk_aug = jnp.concatenate([ht[28:32], hot], axis=-1)
    bh_map = lambda bh, i: (bh % N_HEADS, bh // N_HEADS, i, 0)
    bh_full = lambda bh, i: (bh % N_HEADS, bh // N_HEADS, 0, 0)
    tq = _q_block(seq)
    o_c = _flash(score_bound(HEAD_DIM, lw["moba_qn_g"], lw["moba_kn_g"]), mq_aug, mk_aug,
                 with_ones(ht[32:36]), q_map=bh_map, kv_map=bh_full,
                 o_map=lambda bh, i: (bh // N_HEADS, bh % N_HEADS, i, 0),
                 grid=(batch * N_HEADS, seq // tq), rows=tq, tq=tq, seq=seq,
                 out_lead=(batch, N_HEADS, seq), out_dtype=BF16, window_tiles=None, name="moba_attention")
    o_c = token_major(o_c)

    ncmp = seq // NSA_CMP_STRIDE
    tok = lambda a: a.reshape(batch, ncmp, NSA_CMP_STRIDE * HEAD_DIM)
    pe = lw["nsa_cmp_pe"]
    pe4 = pe.reshape(4, NSA_CMP_STRIDE * HEAD_DIM)
    wc = pad64(lw["nsa_cmp_w"]).astype(BF16)
    gk = pad64(lw["nsa_kn_g"][0].reshape(1, HEAD_DIM))
    kc, vc = _nsa_compress(tok(ht[40]), tok(ht[41]), pe4, wc[0], wc[1], gk, consts["tabs_cmp"], batch)

    nq4 = pad64(ht[36:40]).reshape(N_HEADS, batch, nq, t, LANES).transpose(1, 2, 0, 3, 4)
    nq4 = nq4.reshape(batch, nq, N_HEADS * t, LANES)
    q_aug, d_c = _nsa_select(nq4, kc, vc, consts["mmat"], batch, seq)

    rows = N_HEADS * t
    b_map = lambda b, i: (b, i, 0, 0)
    b_full = lambda b, i: (b, 0, 0)
    nsa_flash = functools.partial(_flash, q_map=b_map, kv_map=b_full, o_map=b_map, grid=(batch, nq),
                                  rows=rows, tq=t, seq=seq, out_lead=(batch, nq, rows), out_dtype=F32)
    ks_aug = jnp.concatenate([ht[42], jnp.broadcast_to(consts["slc_hot"], (batch, seq, LANES - HEAD_DIM))],
                             axis=-1)
    d_s = nsa_flash(score_bound(HEAD_DIM, lw["nsa_qn_g"], lw["nsa_kn_g"][1]), q_aug, ks_aug,
                    with_ones(ht[43]), window_tiles=None, name="nsa_selected")
    d_w = nsa_flash(score_bound(HEAD_DIM, lw["nsa_qn_g"], lw["nsa_kn_g"][2]), nq4, pad64(ht[44]),
                    with_ones(ht[45]), window_tiles=NSA_WINDOW // t, name="nsa_window")
    stacked = lambda o: (o.reshape(batch, nq, N_HEADS, t, HEAD_DIM).transpose(0, 1, 3, 2, 4)
                         .reshape(n, MIX_WIDTH))
    return h, o_a, o_b, o_c, stacked(d_c), stacked(d_s), stacked(d_w)


def _layer(x2, lw, consts, batch, seq, lam_init):
    n, d = x2.shape
    h, o_a, o_b, o_c, d_c, d_s, d_w = _mixers(x2, lw, consts, batch, seq, lam_init)
    x2 = _merge(x2, lw["attn_norm_g"].reshape(1, d), o_a, o_b, o_c, d_c, d_s, d_w,
                h, consts["emat"], lw["w_gate"].astype(BF16), lw["b_gate"].reshape(4, 1, d),
                lw["w_branch"].astype(BF16), lw["w_out"].astype(BF16), _row_tile(n, seq, 512))
    return _ffn(x2, lw["ffn_norm_g"].reshape(1, d), lw["w_ffn_gate"].astype(BF16),
                lw["w_ffn_up"].astype(BF16), lw["w_ffn_down"].astype(BF16), _row_tile(n, seq, 256))


def kernel(x, attn_norm_g, w_in, diff_qn_g, diff_kn_g, diff_lam, diff_subln_g, moba_qn_g, moba_kn_g,
           nsa_qn_g, nsa_kn_g, nsa_cmp_pe, nsa_cmp_w, w_gate, b_gate, w_branch, w_out, ffn_norm_g,
           w_ffn_gate, w_ffn_up, w_ffn_down):
    batch, seq, d = x.shape
    weights = dict(attn_norm_g=attn_norm_g, w_in=w_in, diff_qn_g=diff_qn_g, diff_kn_g=diff_kn_g,
                   diff_lam=diff_lam, diff_subln_g=diff_subln_g, moba_qn_g=moba_qn_g, moba_kn_g=moba_kn_g,
                   nsa_qn_g=nsa_qn_g, nsa_kn_g=nsa_kn_g, nsa_cmp_pe=nsa_cmp_pe, nsa_cmp_w=nsa_cmp_w,
                   w_gate=w_gate, b_gate=b_gate, w_branch=w_branch, w_out=w_out, ffn_norm_g=ffn_norm_g,
                   w_ffn_gate=w_ffn_gate, w_ffn_up=w_ffn_up, w_ffn_down=w_ffn_down)
    consts = _layer_constants(seq)
    x2 = x.reshape(batch * seq, d)
    for layer in range(w_in.shape[0]):
        lw = {k: v[layer] for k, v in weights.items()}
        lam_init = 0.8 - 0.6 * math.exp(-0.3 * layer)
        x2 = _layer(x2, lw, consts, batch, seq, lam_init)
    return x2.reshape(batch, seq, d)
```

```python
import functools
import math

import numpy as np
import jax
import jax.numpy as jnp
from jax import lax
from jax.experimental import pallas as pl
from jax.experimental.pallas import tpu as pltpu

HEAD_DIM = 64
N_HEADS = 4
MIX_WIDTH = N_HEADS * HEAD_DIM
ROPE_THETA = 500000.0
ROPE_FRACTION_DEN = 4
DIFF_DIM = HEAD_DIM // 2
MOBA_BLOCK = 256
MOBA_TOPK = 3
NSA_CMP_LEN = 32
NSA_CMP_STRIDE = 16
NSA_SLC_BLOCK = 64
NSA_TOPN = 16
NSA_WINDOW = 512
NEG_INF = -1e30
BIG = 1e30
RMS_EPS = 1e-6

LANES = 128
ATT_TILE = 256
UNROLL = 4
GROUP = 256
N_GROUPS = 12
VMEM_LIMIT = 56 * 1024 * 1024
LOG2_E = math.log2(math.e)
EXP2_LIMIT = 80.0
NORM_SLACK = 1.05

F32 = jnp.float32
BF16 = jnp.bfloat16

_NT = (((1,), (1,)), ((), ()))


def _dot(a, b):
    return jnp.dot(a, b, preferred_element_type=F32)


def _dot_nt(a, b):
    return lax.dot_general(a, b, _NT, preferred_element_type=F32)


def _split_bf16(x):
    hi = x.astype(BF16)
    lo = (x - hi.astype(F32)).astype(BF16)
    return hi, lo


def _params(*sem):
    return pltpu.CompilerParams(dimension_semantics=sem, vmem_limit_bytes=VMEM_LIMIT)


def _resident(shape, index_map):
    return pl.BlockSpec(shape, index_map, pipeline_mode=pl.Buffered(1))


def _rope_tables(pos, d, width):
    r = d // ROPE_FRACTION_DEN
    half = r // 2
    inv = ROPE_THETA ** (-jnp.arange(half, dtype=F32) * 2.0 / r)
    ang = pos.astype(F32)[:, None] * inv[None, :]
    cos, sin = jnp.cos(ang), jnp.sin(ang)
    n = pos.shape[0]
    one = jnp.ones((n, d - r), F32)
    zero_h = jnp.zeros((n, half), F32)
    zero_t = jnp.zeros((n, d - r), F32)
    cos_g = jnp.concatenate([cos, cos, one], axis=1)
    sa_g = jnp.concatenate([-sin, zero_h, zero_t], axis=1)
    sb_g = jnp.concatenate([zero_h, sin, zero_t], axis=1)
    reps = width // d
    return (jnp.tile(cos_g, (1, reps)), jnp.tile(sa_g, (1, reps)), jnp.tile(sb_g, (1, reps)))


def _group_mean_matrix(d, width):
    g = np.arange(width) // d
    return jnp.asarray((g[:, None] == g[None, :]).astype(np.float32) / d, BF16)


def _rope_apply(y, cos, sa, sb, half):
    w = y.shape[-1]
    return y * cos + pltpu.roll(y, w - half, 1) * sa + pltpu.roll(y, half, 1) * sb


_GROUP_CFG = (
    (None, None), (None, None), (None, None),
    (DIFF_DIM, DIFF_DIM), (DIFF_DIM, DIFF_DIM), (None, None),
    (HEAD_DIM, HEAD_DIM), (HEAD_DIM, HEAD_DIM), (None, None),
    (HEAD_DIM, HEAD_DIM),
    (HEAD_DIM, HEAD_DIM),
    (HEAD_DIM, HEAD_DIM),
)


def _inproj_kernel(x_ref, g_ref, w_ref, gain_ref, nflag_ref, rflag_ref, b64_ref, b32_ref,
                   c64_ref, sa64_ref, sb64_ref, c32_ref, sa32_ref, sb32_ref, mhot_ref, shot_ref,
                   sbq, sbk, sbv, dfq, dfk, dfv, mbq, mbk, mbv, nsq, kct, vct, ksa, vs1, kwp, vw1, glog):
    x = x_ref[...]
    tm = x.shape[0]
    t = ATT_TILE
    hd = HEAD_DIM
    xn = x * lax.rsqrt(jnp.mean(x * x, axis=-1, keepdims=True) + RMS_EPS) * g_ref[...]
    xn = xn.astype(BF16)
    zeros = jnp.zeros((tm, hd), BF16)
    ones_lane = jnp.where(lax.broadcasted_iota(jnp.int32, (tm, hd), 1) == 0, 1.0, 0.0).astype(BF16)

    def project(gi):
        norm, rope = _GROUP_CFG[gi]
        y = _dot(xn, w_ref[:, gi * GROUP:(gi + 1) * GROUP])
        gain = gain_ref[gi]
        if norm is not None:
            bmat = b64_ref[...] if norm == HEAD_DIM else b32_ref[...]
            ms = _dot((y * y).astype(BF16), bmat)
            y = y * jnp.where(nflag_ref[gi] > 0, lax.rsqrt(ms + RMS_EPS) * gain, gain)
        else:
            y = y * gain
        if rope is not None:
            if rope == HEAD_DIM:
                cos, sa, sb = c64_ref[...], sa64_ref[...], sb64_ref[...]
            else:
                cos, sa, sb = c32_ref[...], sa32_ref[...], sb32_ref[...]
            rf = rflag_ref[gi]
            cos = jnp.where(rf > 0, cos, 1.0)
            y = _rope_apply(y, cos, sa * rf, sb * rf, rope // ROPE_FRACTION_DEN // 2)
        return y

    def quarters(y):
        return [y[:, h * hd:(h + 1) * hd].astype(BF16) for h in range(N_HEADS)]

    def put_heads(ref, gi, extra):
        for h, piece in enumerate(quarters(project(gi))):
            if extra is None:
                ref[h] = piece
            else:
                ref[h, :, 0:hd] = piece
                ref[h, :, hd:] = extra

    put_heads(sbq, 0, None)
    put_heads(sbk, 1, None)
    put_heads(sbv, 2, None)
    put_heads(dfq, 3, None)
    put_heads(dfk, 4, None)
    put_heads(dfv, 5, ones_lane)
    put_heads(mbq, 6, zeros)
    put_heads(mbk, 7, mhot_ref[...])
    put_heads(mbv, 8, ones_lane)

    for h, piece in enumerate(quarters(project(9))):
        for qt in range(tm // t):
            nsq[qt, h * t:(h + 1) * t, 0:hd] = piece[qt * t:(qt + 1) * t]
            nsq[qt, h * t:(h + 1) * t, hd:] = zeros[0:t]

    kc_tok, vc_tok, k_slc, v_slc = quarters(project(10))
    kct[...] = kc_tok
    vct[...] = vc_tok
    ksa[:, 0:hd] = k_slc
    ksa[:, hd:] = shot_ref[...]
    vs1[:, 0:hd] = v_slc
    vs1[:, hd:] = ones_lane

    y = project(11)
    kwp[:, 0:hd] = y[:, 0:hd].astype(BF16)
    kwp[:, hd:] = zeros
    vw1[:, 0:hd] = y[:, hd:2 * hd].astype(BF16)
    vw1[:, hd:] = ones_lane
    glog[...] = y[:, 2 * hd:].astype(BF16)


def _inproj(x2, g, w, gain, nflag, rflag, consts, batch, seq, tm):
    n, d = x2.shape
    ncols = N_GROUPS * GROUP
    nst = seq // tm
    t = ATT_TILE
    row = lambda i: (i, 0)
    fix2 = lambda i: (0, 0)
    fix3 = lambda i: (0, 0, 0)
    tab = lambda i: (i % nst, 0)
    vec = _resident((N_GROUPS, 1, GROUP), fix3)
    tspec = pl.BlockSpec((tm, GROUP), tab)
    hspec = pl.BlockSpec((tm, HEAD_DIM), tab)

    def heads_out(width):
        return (pl.BlockSpec((None, N_HEADS, tm, width), lambda i: (i // nst, 0, i % nst, 0)),
                jax.ShapeDtypeStruct((batch, N_HEADS, seq, width), BF16))

    def tokens_out(width):
        return (pl.BlockSpec((None, tm, width), lambda i: (i // nst, i % nst, 0)),
                jax.ShapeDtypeStruct((batch, seq, width), BF16))

    stacked_q = (pl.BlockSpec((None, tm // t, N_HEADS * t, LANES), lambda i: (i // nst, i % nst, 0, 0)),
                 jax.ShapeDtypeStruct((batch, seq // t, N_HEADS * t, LANES), BF16))
    outs = ([heads_out(HEAD_DIM)] * 5 + [heads_out(LANES)] * 4 + [stacked_q]
            + [tokens_out(HEAD_DIM)] * 2 + [tokens_out(LANES)] * 5)
    return pl.pallas_call(
        _inproj_kernel,
        grid=(n // tm,),
        in_specs=[pl.BlockSpec((tm, d), row), _resident((1, d), fix2), _resident((d, ncols), fix2),
                  vec, vec, vec, _resident((GROUP, GROUP), fix2), _resident((GROUP, GROUP), fix2),
                  tspec, tspec, tspec, tspec, tspec, tspec, hspec, hspec],
        out_specs=[o[0] for o in outs],
        out_shape=[o[1] for o in outs],
        compiler_params=_params("parallel"),
        name="inproj",
    )(x2, g, w, gain, nflag, rflag, _group_mean_matrix(HEAD_DIM, GROUP),
      _group_mean_matrix(DIFF_DIM, GROUP), *consts["tabs64"], *consts["tabs32"],
      consts["moba_hot"], consts["slc_hot"])


def _col_minus_row(nrows, tq=None):
    shape = (nrows, ATT_TILE)
    row = lax.broadcasted_iota(jnp.int32, shape, 0)
    if tq is not None:
        row = row & (tq - 1)
    return lax.broadcasted_iota(jnp.int32, shape, 1) - row


def _causal_sweep(step, i, rows, tq, strict):
    t = ATT_TILE
    per_block = tq // t
    base = i * per_block
    tile = lambda j: pl.multiple_of(j * t, t)
    visible = (lambda cmr: cmr < 0) if strict else (lambda cmr: cmr <= 0)
    if tq == t:
        step(slice(0, rows), [tile(base)], visible(_col_minus_row(rows, tq)))
    else:
        assert rows == tq
        own = visible(_col_minus_row(t))
        for d in range(per_block - 1, -1, -1):
            step(slice(d * t, (d + 1) * t), [tile(base + d)], own)
            if (d + 1) * t < rows:
                step(slice((d + 1) * t, rows), [tile(base + d)], None)

    groups = base // UNROLL

    def body(g, carry):
        newest = base - 1 - g * UNROLL
        step(slice(0, rows), [tile(newest - u) for u in range(UNROLL)], None)
        return carry

    lax.fori_loop(0, groups, body, 0)
    left = base - groups * UNROLL
    for r in range(UNROLL - 1, 0, -1):
        @pl.when(left >= r)
        def _():
            step(slice(0, rows), [tile(r - 1)], None)


def _sb_kernel(q_ref, k_ref, v_ref, o_ref, carry_ref, acc_ref, *, tq):
    t = ATT_TILE
    i = pl.program_id(1)
    rows = q_ref.shape[0]
    r = lax.broadcasted_iota(jnp.int32, (t, t), 0)
    c = lax.broadcasted_iota(jnp.int32, (t, t), 1)
    suffix = (r > c).astype(BF16)

    def step(rs, starts, mask):
        n = rs.stop - rs.start
        q = q_ref[rs, :]
        carry = carry_ref[rs, :]
        pv = None
        for start in starts:
            z = _dot_nt(q, k_ref[pl.ds(start, t), :])
            sp = jnp.maximum(z, 0.0) + jnp.log(1.0 + jnp.exp2(jnp.abs(z) * (-LOG2_E)))
            if mask is not None:
                sp = jnp.where(mask, sp, 0.0)
            hi, lo = _split_bf16(sp)
            both = _dot(jnp.concatenate([hi, lo], axis=0), suffix)
            later = both[:n] + both[n:]
            a = jnp.exp2((z - sp - later - carry) * LOG2_E)
            if mask is not None:
                a = jnp.where(mask, a, 0.0)
            d = _dot(a.astype(BF16), v_ref[pl.ds(start, t), :])
            pv = d if pv is None else pv + d
            carry = carry + (later[:, 0:1] + sp[:, 0:1])
        acc_ref[rs, :] += pv
        carry_ref[rs, :] = carry

    carry_ref[...] = jnp.zeros_like(carry_ref)
    acc_ref[...] = jnp.zeros_like(acc_ref)
    _causal_sweep(step, i, rows, tq, True)
    o_ref[...] = acc_ref[...].astype(o_ref.dtype)


def _q_block(seq):
    return min(4 * ATT_TILE, seq)


def _head_block(rows, width):
    if rows is None:
        return lambda seq: pl.BlockSpec((None, None, seq, width),
                                        lambda bh, i: (bh // N_HEADS, bh % N_HEADS, 0, 0))
    return pl.BlockSpec((None, None, rows, width), lambda bh, i: (bh // N_HEADS, bh % N_HEADS, i, 0))


def _stick_breaking(q, k, v, batch, seq):
    tq = _q_block(seq)
    return pl.pallas_call(
        functools.partial(_sb_kernel, tq=tq),
        grid=(batch * N_HEADS, seq // tq),
        in_specs=[_head_block(tq, HEAD_DIM), _head_block(None, HEAD_DIM)(seq), _head_block(None, HEAD_DIM)(seq)],
        out_specs=_head_block(tq, HEAD_DIM),
        out_shape=jax.ShapeDtypeStruct((batch, N_HEADS, seq, HEAD_DIM), BF16),
        scratch_shapes=[pltpu.VMEM((tq, 1), F32), pltpu.VMEM((tq, HEAD_DIM), F32)],
        compiler_params=_params("parallel", "arbitrary"),
        name="stick_breaking",
    )(q, k, v)


def _softmax_step(q, k_ref, v_ref, starts, mask, rs, m_ref, acc_ref):
    t = ATT_TILE
    pv = None
    for start in starts:
        s = _dot_nt(q, k_ref[pl.ds(start, t), :])
        if mask is not None:
            s = jnp.where(mask, s, NEG_INF)
        v1 = v_ref[pl.ds(start, t), :]
        if m_ref is None:
            d = _dot(jnp.exp2(s).astype(BF16), v1)
            pv = d if pv is None else pv + d
            continue
        m_old = m_ref[rs, :]
        mn = jnp.maximum(m_old, jnp.max(s, axis=1, keepdims=True))
        acc_ref[rs, :] = jnp.exp2(m_old - mn) * acc_ref[rs, :] + _dot(jnp.exp2(s - mn).astype(BF16), v1)
        m_ref[rs, :] = mn
    if m_ref is None:
        acc_ref[rs, :] += pv


def _either_path(bound_ref, m_refs, sweep):
    bounded = bound_ref[0] <= EXP2_LIMIT

    @pl.when(bounded)
    def _():
        sweep((None,) * len(m_refs))

    @pl.when(jnp.logical_not(bounded))
    def _():
        for ref in m_refs:
            ref[...] = jnp.full(ref.shape, NEG_INF, F32)
        sweep(m_refs)


def _normalised(acc):
    return acc[:, :HEAD_DIM] / acc[:, HEAD_DIM:HEAD_DIM + 1]


def _diff_kernel(bound_ref, q_ref, k_ref, v_ref, lam_ref, g_ref, o_ref, q1_ref, q2_ref, m1_ref, m2_ref,
                 a1_ref, a2_ref, *, lam_init, tq):
    i = pl.program_id(1)
    rows = q_ref.shape[0]
    q = q_ref[...]
    lane = lax.broadcasted_iota(jnp.int32, q.shape, 1)
    zero = jnp.zeros_like(q)
    q1_ref[...] = jnp.where(lane < DIFF_DIM, q, zero)
    q2_ref[...] = jnp.where(lane >= DIFF_DIM, q, zero)
    for ref in (a1_ref, a2_ref):
        ref[...] = jnp.zeros_like(ref)

    def sweep(m_refs):
        def step(rs, starts, mask):
            _softmax_step(q1_ref[rs, :], k_ref, v_ref, starts, mask, rs, m_refs[0], a1_ref)
            _softmax_step(q2_ref[rs, :], k_ref, v_ref, starts, mask, rs, m_refs[1], a2_ref)

        _causal_sweep(step, i, rows, tq, False)

    _either_path(bound_ref, (m1_ref, m2_ref), sweep)

    lp = lam_ref[...]
    lam = (jnp.exp(jnp.sum(lp[0:1, :] * lp[1:2, :], axis=1, keepdims=True))
           - jnp.exp(jnp.sum(lp[2:3, :] * lp[3:4, :], axis=1, keepdims=True)) + lam_init)
    o = _normalised(a1_ref[...]) - lam * _normalised(a2_ref[...])
    o = o * lax.rsqrt(jnp.mean(o * o, axis=-1, keepdims=True) + RMS_EPS) * g_ref[...]
    o_ref[...] = (o * (1.0 - lam_init)).astype(o_ref.dtype)


_SCALAR = pl.BlockSpec(memory_space=pltpu.SMEM)


def _diff_attention(bound, q, k, v1, lam_p, subln_g, batch, seq, lam_init):
    tq = _q_block(seq)
    fix = lambda bh, i: (0, 0)
    return pl.pallas_call(
        functools.partial(_diff_kernel, lam_init=lam_init, tq=tq),
        grid=(batch * N_HEADS, seq // tq),
        in_specs=[_SCALAR, _head_block(tq, HEAD_DIM), _head_block(None, HEAD_DIM)(seq),
                  _head_block(None, LANES)(seq),
                  pl.BlockSpec((4, DIFF_DIM), fix), pl.BlockSpec((1, HEAD_DIM), fix)],
        out_specs=_head_block(tq, HEAD_DIM),
        out_shape=jax.ShapeDtypeStruct((batch, N_HEADS, seq, HEAD_DIM), BF16),
        scratch_shapes=[pltpu.VMEM((tq, HEAD_DIM), BF16), pltpu.VMEM((tq, HEAD_DIM), BF16),
                        pltpu.VMEM((tq, 1), F32), pltpu.VMEM((tq, 1), F32),
                        pltpu.VMEM((tq, LANES), F32), pltpu.VMEM((tq, LANES), F32)],
        compiler_params=_params("parallel", "arbitrary"),
        name="diff_attention",
    )(bound, q, k, v1, lam_p, subln_g)


def _select_bias(score_t, keep_extra_t, ntop):
    nblk = score_t.shape[0]
    sub = lax.broadcasted_iota(jnp.int32, score_t.shape, 0)
    rank = jnp.zeros(score_t.shape, jnp.int32)
    for b in range(nblk):
        row = score_t[b:b + 1, :]
        ahead = (row > score_t) | ((row == score_t) & (sub > b))
        rank = rank + ahead.astype(jnp.int32)
    keep = rank < ntop
    if keep_extra_t is not None:
        keep = keep_extra_t(keep, sub)
    return jnp.where(keep, 0.0, NEG_INF)


def _bias_lanes(bias_t, lane0):
    nblk, t = bias_t.shape
    parts = []
    if lane0:
        parts.append(jnp.zeros((lane0, t), F32))
    parts.append(bias_t)
    if LANES - lane0 - nblk:
        parts.append(jnp.zeros((LANES - lane0 - nblk, t), F32))
    return jnp.concatenate(parts, axis=0).T


def _moba_gate_kernel(q_ref, k_ref, a_ref, o_ref, km_hi, km_lo, *, nblk):
    i = pl.program_id(1)

    @pl.when(i == 0)
    def _():
        kmean = _dot(a_ref[...], k_ref[...])
        hi, lo = _split_bf16(kmean)
        km_hi[...] = hi
        km_lo[...] = lo

    q = q_ref[...]
    rows = q.shape[0]
    gate = _dot_nt(q, km_hi[...]) + _dot_nt(q, km_lo[...])
    gate_t = gate.T[HEAD_DIM:HEAD_DIM + nblk, :]
    blk = lax.broadcasted_iota(jnp.int32, gate_t.shape, 0)
    own = (i * rows + lax.broadcasted_iota(jnp.int32, gate_t.shape, 1)) // MOBA_BLOCK
    score_t = jnp.where(blk < own, gate_t, NEG_INF)
    ksel = min(MOBA_TOPK, nblk - 1)

    def keep_rule(keep, sub):
        return (keep & (sub < own)) | (sub == own)

    bias_t = _select_bias(score_t, keep_rule, ksel)
    o_ref[...] = (q.astype(F32) + _bias_lanes(bias_t, HEAD_DIM)).astype(o_ref.dtype)


def _moba_gate(q_pad, k_aug, amat, batch, seq):
    tq = _q_block(seq)
    nblk = seq // MOBA_BLOCK
    return pl.pallas_call(
        functools.partial(_moba_gate_kernel, nblk=nblk),
        grid=(batch * N_HEADS, seq // tq),
        in_specs=[_head_block(tq, LANES), _head_block(None, LANES)(seq),
                  pl.BlockSpec((LANES, seq), lambda bh, i: (0, 0))],
        out_specs=_head_block(tq, LANES),
        out_shape=jax.ShapeDtypeStruct((batch, N_HEADS, seq, LANES), BF16),
        scratch_shapes=[pltpu.VMEM((LANES, LANES), BF16), pltpu.VMEM((LANES, LANES), BF16)],
        compiler_params=_params("parallel", "arbitrary"),
        name="moba_gate",
    )(q_pad, k_aug, amat)


def _flash_kernel(bound_ref, q_ref, k_ref, v_ref, o_ref, m_ref, acc_ref, *, tq, window_tiles):
    t = ATT_TILE
    i = pl.program_id(1)
    rows = q_ref.shape[0]
    acc_ref[...] = jnp.zeros_like(acc_ref)

    def sweep(m_refs):
        def step(rs, starts, mask):
            _softmax_step(q_ref[rs, :], k_ref, v_ref, starts, mask, rs, m_refs[0], acc_ref)

        if window_tiles is None:
            _causal_sweep(step, i, rows, tq, False)
            return
        assert tq == t
        step(slice(0, rows), [pl.multiple_of(i * t, t)], _col_minus_row(rows, tq) <= 0)
        for back in range(1, window_tiles + 1):
            @pl.when(i >= back)
            def _():
                mask = (_col_minus_row(rows, tq) > 0) if back == window_tiles else None
                step(slice(0, rows), [pl.multiple_of((i - back) * t, t)], mask)

    _either_path(bound_ref, (m_ref,), sweep)
    o_ref[...] = _normalised(acc_ref[...]).astype(o_ref.dtype)


def _flash(bound, q, k, v1, *, q_map, kv_map, o_map, grid, rows, tq, seq, out_lead, out_dtype, window_tiles,
           name):
    lead = (None,) * (q.ndim - 2)
    klead = (None,) * (k.ndim - 2)
    return pl.pallas_call(
        functools.partial(_flash_kernel, tq=tq, window_tiles=window_tiles),
        grid=grid,
        in_specs=[_SCALAR, pl.BlockSpec(lead + (rows, LANES), q_map),
                  pl.BlockSpec(klead + (seq, LANES), kv_map),
                  pl.BlockSpec(klead + (seq, LANES), kv_map)],
        out_specs=pl.BlockSpec(lead + (rows, HEAD_DIM), o_map),
        out_shape=jax.ShapeDtypeStruct(out_lead + (HEAD_DIM,), out_dtype),
        scratch_shapes=[pltpu.VMEM((rows, 1), F32), pltpu.VMEM((rows, LANES), F32)],
        compiler_params=_params("parallel", "arbitrary"),
        name=name,
    )(bound, q, k, v1)


def _nsa_compress_kernel(tk_ref, tv_ref, pe_ref, wk_ref, wv_ref, g_ref, cos_ref, sa_ref, sb_ref,
                         kc_ref, vc_ref):
    rows = tk_ref.shape[0]

    def compress(tok, pe_lo, pe_hi, w_ref):
        half = w_ref.shape[0] // 2
        a = _dot((tok + pe_lo).astype(BF16), w_ref[0:half, :])
        b = _dot((tok + pe_hi).astype(BF16), w_ref[half:, :])
        return a + pltpu.roll(b, rows - 1, 0)

    kc = compress(tk_ref[...].astype(F32), pe_ref[0:1, :], pe_ref[1:2, :], wk_ref)
    vc = compress(tv_ref[...].astype(F32), pe_ref[2:3, :], pe_ref[3:4, :], wv_ref)
    ms = jnp.sum(kc * kc, axis=-1, keepdims=True) / HEAD_DIM
    kc = kc * lax.rsqrt(ms + RMS_EPS) * g_ref[...]
    kc = _rope_apply(kc, cos_ref[...], sa_ref[...], sb_ref[...], HEAD_DIM // ROPE_FRACTION_DEN // 2)
    kc_ref[...] = kc.astype(kc_ref.dtype)
    vc_ref[...] = vc.astype(vc_ref.dtype)


def _nsa_compress(tk, tv, pe4, wk, wv, gk, tabs, batch):
    rows, width = tk.shape[1], tk.shape[2]
    b3 = lambda b: (b, 0, 0)
    fix = lambda b: (0, 0)
    tok = pl.BlockSpec((None, rows, width), b3)
    out = pl.BlockSpec((None, rows, LANES), b3)
    tab = pl.BlockSpec((rows, LANES), fix)
    return pl.pallas_call(
        _nsa_compress_kernel,
        grid=(batch,),
        in_specs=[tok, tok, pl.BlockSpec((4, width), fix),
                  pl.BlockSpec((2 * width, LANES), fix), pl.BlockSpec((2 * width, LANES), fix),
                  pl.BlockSpec((1, LANES), fix), tab, tab, tab],
        out_specs=[out, out],
        out_shape=[jax.ShapeDtypeStruct((batch, rows, LANES), BF16)] * 2,
        compiler_params=_params("parallel"),
        name="nsa_compress",
    )(tk, tv, pe4, wk, wv, gk, *tabs)


def _nsa_select_kernel(q_ref, kc_ref, vc_ref, m_ref, qa_ref, oc_ref, *, nslc, ntop):
    t = ATT_TILE
    i = pl.program_id(1)
    q = q_ref[...]
    rows = q.shape[0]
    ncmp = kc_ref.shape[0]
    s = _dot_nt(q, kc_ref[...])
    qpos = i * t + (lax.broadcasted_iota(jnp.int32, (rows, ncmp), 0) & (t - 1))
    cmp_end = NSA_CMP_STRIDE * lax.broadcasted_iota(jnp.int32, (rows, ncmp), 1) + (NSA_CMP_LEN - 1)
    mask = cmp_end <= qpos
    s = jnp.where(mask, s, NEG_INF)
    e = jnp.where(mask, jnp.exp2(s - jnp.max(s, axis=1, keepdims=True)), 0.0)
    p = e / jnp.maximum(jnp.sum(e, axis=1, keepdims=True), 1e-30)
    oc_ref[...] = _dot(p.astype(BF16), vc_ref[...])[:, :HEAD_DIM].astype(oc_ref.dtype)

    psum = p[0:t]
    for h in range(1, rows // t):
        psum = psum + p[h * t:(h + 1) * t]
    hi, lo = _split_bf16(psum)
    imp = _dot(hi, m_ref[...]) + _dot(lo, m_ref[...])
    imp_t = imp.T[HEAD_DIM:HEAD_DIM + nslc, :]
    blk = lax.broadcasted_iota(jnp.int32, imp_t.shape, 0)
    cur = (i * t + lax.broadcasted_iota(jnp.int32, imp_t.shape, 1)) // NSA_SLC_BLOCK
    forced = (blk == 0) | (blk == cur) | (blk == cur - 1)
    allowed = blk <= cur
    score_t = jnp.where(allowed, jnp.where(forced, BIG, imp_t), NEG_INF)
    bias = _bias_lanes(_select_bias(score_t, None, ntop), HEAD_DIM)
    qf = q.astype(F32)
    for h in range(rows // t):
        qa_ref[h * t:(h + 1) * t, :] = (qf[h * t:(h + 1) * t] + bias).astype(qa_ref.dtype)


def _nsa_select(q4, kc, vc, mmat, batch, seq):
    t = ATT_TILE
    rows = N_HEADS * t
    ncmp = kc.shape[1]
    nslc = seq // NSA_SLC_BLOCK
    qmap = lambda b, i: (b, i, 0, 0)
    kmap = lambda b, i: (b, 0, 0)
    return pl.pallas_call(
        functools.partial(_nsa_select_kernel, nslc=nslc, ntop=min(NSA_TOPN, nslc)),
        grid=(batch, seq // t),
        in_specs=[pl.BlockSpec((None, None, rows, LANES), qmap),
                  pl.BlockSpec((None, ncmp, LANES), kmap), pl.BlockSpec((None, ncmp, LANES), kmap),
                  pl.BlockSpec((ncmp, LANES), lambda b, i: (0, 0))],
        out_specs=[pl.BlockSpec((None, None, rows, LANES), qmap),
                   pl.BlockSpec((None, None, rows, HEAD_DIM), qmap)],
        out_shape=[jax.ShapeDtypeStruct((batch, seq // t, rows, LANES), BF16),
                   jax.ShapeDtypeStruct((batch, seq // t, rows, HEAD_DIM), F32)],
        compiler_params=_params("parallel", "arbitrary"),
        name="nsa_select",
    )(q4, kc, vc, mmat)


def _merge_kernel(x_ref, g_ref, oa_ref, ob_ref, oc_ref, dc_ref, ds_ref, dw_ref, gl_ref, e_ref,
                  wg_ref, bg_ref, wb_ref, wo_ref, y_ref):
    t = ATT_TILE
    x = x_ref[...]
    xn = x * lax.rsqrt(jnp.mean(x * x, axis=-1, keepdims=True) + RMS_EPS) * g_ref[...]
    xn = xn.astype(BF16)

    def token_major(ref):
        return jnp.concatenate([ref[h] for h in range(N_HEADS)], axis=1)

    def unstack(ref):
        return jnp.concatenate(
            [jnp.concatenate([ref[qt, h * t:(h + 1) * t, :] for h in range(N_HEADS)], axis=1)
             for qt in range(ref.shape[0])], axis=0)

    sig = 1.0 / (1.0 + jnp.exp(-gl_ref[...].astype(F32)))
    hi, lo = _split_bf16(sig)
    gexp = _dot(hi, e_ref[...]) + _dot(lo, e_ref[...])
    w = MIX_WIDTH
    o_d = (gexp[:, 0:w] * unstack(dc_ref) + gexp[:, w:2 * w] * unstack(ds_ref)
           + gexp[:, 2 * w:3 * w] * unstack(dw_ref)).astype(BF16)
    merged = None
    for bi, o in enumerate((token_major(oa_ref), token_major(ob_ref), token_major(oc_ref), o_d)):
        gate = 1.0 / (1.0 + jnp.exp(-(_dot(xn, wg_ref[bi]) + bg_ref[bi])))
        term = gate * _dot(o, wb_ref[bi])
        merged = term if merged is None else merged + term
    y_ref[...] = x + _dot(merged.astype(BF16), wo_ref[...])


def _merge(x2, g, oa, ob, oc, dc, ds, dw, glog, emat, wg, bg, wb, wo, seq, tm):
    n, d = x2.shape
    nst = seq // tm
    t = ATT_TILE
    row = lambda i: (i, 0)
    fix2 = lambda i: (0, 0)
    fix3 = lambda i: (0, 0, 0)
    heads = pl.BlockSpec((None, N_HEADS, tm, HEAD_DIM), lambda i: (i // nst, 0, i % nst, 0))
    stacked = pl.BlockSpec((None, tm // t, N_HEADS * t, HEAD_DIM), lambda i: (i // nst, i % nst, 0, 0))
    return pl.pallas_call(
        _merge_kernel,
        grid=(n // tm,),
        in_specs=[pl.BlockSpec((tm, d), row), _resident((1, d), fix2), heads, heads, heads,
                  stacked, stacked, stacked,
                  pl.BlockSpec((None, tm, LANES), lambda i: (i // nst, i % nst, 0)),
                  _resident((LANES, 3 * MIX_WIDTH), fix2),
                  _resident((4, d, d), fix3), _resident((4, 1, d), fix3),
                  _resident((4, MIX_WIDTH, d), fix3), _resident((d, d), fix2)],
        out_specs=pl.BlockSpec((tm, d), row),
        out_shape=jax.ShapeDtypeStruct((n, d), F32),
        compiler_params=_params("parallel"),
        name="gated_merge",
    )(x2, g, oa, ob, oc, dc, ds, dw, glog, emat, wg, bg, wb, wo)


def _ffn_kernel(x_ref, g_ref, wg_ref, wu_ref, wd_ref, y_ref):
    x = x_ref[...]
    hn = x * lax.rsqrt(jnp.mean(x * x, axis=-1, keepdims=True) + RMS_EPS) * g_ref[...]
    hn = hn.astype(BF16)
    f = wg_ref.shape[1]
    fc = f // 2 if f % (2 * LANES) == 0 else f
    y = x
    for c in range(0, f, fc):
        gate = _dot(hn, wg_ref[:, c:c + fc])
        up = _dot(hn, wu_ref[:, c:c + fc])
        act = gate * (1.0 / (1.0 + jnp.exp(-gate))) * up
        y = y + _dot(act.astype(BF16), wd_ref[c:c + fc, :])
    y_ref[...] = y


def _ffn(x2, g, wg, wu, wd, tm):
    n, d = x2.shape
    f = wg.shape[1]
    row = lambda i: (i, 0)
    fix = lambda i: (0, 0)
    return pl.pallas_call(
        _ffn_kernel,
        grid=(n // tm,),
        in_specs=[pl.BlockSpec((tm, d), row), _resident((1, d), fix), _resident((d, f), fix),
                  _resident((d, f), fix), _resident((f, d), fix)],
        out_specs=pl.BlockSpec((tm, d), row),
        out_shape=jax.ShapeDtypeStruct((n, d), F32),
        compiler_params=_params("parallel"),
        name="swiglu",
    )(x2, g, wg, wu, wd)


def _layer_constants(seq):
    t = ATT_TILE
    pos = jnp.arange(seq, dtype=jnp.int32)
    tabs64 = _rope_tables(pos, HEAD_DIM, GROUP)
    tabs32 = _rope_tables(pos, DIFF_DIM, GROUP)
    ncmp = seq // NSA_CMP_STRIDE
    cmp_end = NSA_CMP_STRIDE * jnp.arange(ncmp, dtype=jnp.int32) + (NSA_CMP_LEN - 1)
    tabs_cmp = _rope_tables(cmp_end, HEAD_DIM, HEAD_DIM)
    tabs_cmp = tuple(jnp.pad(a, ((0, 0), (0, LANES - HEAD_DIM)), constant_values=c)
                     for a, c in zip(tabs_cmp, (1.0, 0.0, 0.0)))

    s = np.arange(seq)
    nblk = seq // MOBA_BLOCK
    nslc = seq // NSA_SLC_BLOCK
    assert nblk <= LANES - HEAD_DIM and nslc <= LANES - HEAD_DIM and seq % t == 0
    moba_hot = np.zeros((seq, LANES - HEAD_DIM), np.float32)
    moba_hot[s, s // MOBA_BLOCK] = 1.0
    slc_hot = np.zeros((seq, LANES - HEAD_DIM), np.float32)
    slc_hot[s, s // NSA_SLC_BLOCK] = 1.0
    amat = np.zeros((LANES, seq), np.float32)
    amat[HEAD_DIM + s // MOBA_BLOCK, s] = 1.0 / MOBA_BLOCK

    nc_real = (seq - NSA_CMP_LEN) // NSA_CMP_STRIDE + 1
    cstart = NSA_CMP_STRIDE * np.arange(ncmp)
    sstart = NSA_SLC_BLOCK * np.arange(nslc)
    overlap = np.clip(np.minimum(cstart[:, None] + NSA_CMP_LEN, sstart[None, :] + NSA_SLC_BLOCK)
                      - np.maximum(cstart[:, None], sstart[None, :]), 0, None)
    mmat = np.zeros((ncmp, LANES), np.float32)
    mmat[:nc_real, HEAD_DIM:HEAD_DIM + nslc] = overlap[:nc_real].astype(np.float32) / NSA_CMP_STRIDE

    emat = np.zeros((LANES, 3 * MIX_WIDTH), np.float32)
    for br in range(3):
        for hd in range(N_HEADS):
            emat[br * N_HEADS + hd,
                 br * MIX_WIDTH + hd * HEAD_DIM: br * MIX_WIDTH + (hd + 1) * HEAD_DIM] = 1.0

    return dict(tabs64=tabs64, tabs32=tabs32, tabs_cmp=tabs_cmp,
                moba_hot=jnp.asarray(moba_hot, BF16), slc_hot=jnp.asarray(slc_hot, BF16),
                amat=jnp.asarray(amat, BF16), mmat=jnp.asarray(mmat, BF16), emat=jnp.asarray(emat, BF16))


def _lane_vectors(diff_qn_g, diff_kn_g, moba_qn_g, moba_kn_g, nsa_qn_g, nsa_kn_g):
    ones = jnp.ones((GROUP,), F32)
    zeros = jnp.zeros((GROUP,), F32)
    sc64_log2 = LOG2_E / math.sqrt(HEAD_DIM)
    sc32_log2 = LOG2_E / math.sqrt(DIFF_DIM)
    quarter = lambda g, k: jnp.concatenate([g if j == k else jnp.ones((HEAD_DIM,), F32) for j in range(4)])
    flag = lambda k: jnp.concatenate([jnp.full((HEAD_DIM,), 1.0 if j == k else 0.0, F32) for j in range(4)])
    gains = [ones / math.sqrt(HEAD_DIM), ones, ones,
             jnp.tile(diff_qn_g, GROUP // DIFF_DIM) * sc32_log2, jnp.tile(diff_kn_g, GROUP // DIFF_DIM), ones,
             jnp.tile(moba_qn_g, N_HEADS) * sc64_log2, jnp.tile(moba_kn_g, N_HEADS), ones,
             jnp.tile(nsa_qn_g, N_HEADS) * sc64_log2,
             quarter(nsa_kn_g[1], 2), quarter(nsa_kn_g[2], 0)]
    flags = [zeros, zeros, zeros, ones, ones, zeros, ones, ones, zeros, ones, flag(2), flag(0)]
    shape = (N_GROUPS, 1, GROUP)
    return jnp.stack(gains).reshape(shape), jnp.stack(flags).reshape(shape), jnp.stack(flags).reshape(shape)


def _row_tile(n, seq, want):
    tm = want if (n % want == 0 and seq % want == 0) else ATT_TILE
    return min(tm, seq)


def _mixers(x2, lw, consts, batch, seq, lam_init):
    n, d = x2.shape
    t = ATT_TILE
    nq = seq // t
    tm = _row_tile(n, seq, 512)

    gain, nflag, rflag = _lane_vectors(lw["diff_qn_g"], lw["diff_kn_g"], lw["moba_qn_g"],
                                       lw["moba_kn_g"], lw["nsa_qn_g"], lw["nsa_kn_g"])
    w_in = jnp.pad(lw["w_in"], ((0, 0), (0, N_GROUPS * GROUP - lw["w_in"].shape[1]))).astype(BF16)
    (sbq, sbk, sbv, dfq, dfk, dfv1, mbq, mbk, mbv1, nsq, kct, vct, ksa, vs1, kwp, vw1, glog) = _inproj(
        x2, lw["attn_norm_g"].reshape(1, d), w_in, gain, nflag, rflag, consts, batch, seq, tm)

    amax = lambda g: jnp.max(jnp.abs(g))
    score_bound = lambda d_head, gq, gk: (math.sqrt(d_head) * LOG2_E * NORM_SLACK * NORM_SLACK
                                          * amax(gq) * amax(gk)).reshape(1).astype(F32)

    o_a = _stick_breaking(sbq, sbk, sbv, batch, seq)
    o_b = _diff_attention(score_bound(DIFF_DIM, lw["diff_qn_g"], lw["diff_kn_g"]), dfq, dfk, dfv1,
                          lw["diff_lam"], lw["diff_subln_g"].reshape(1, HEAD_DIM), batch, seq, lam_init)

    mq_aug = _moba_gate(mbq, mbk, consts["amat"], batch, seq)
    bh_map = lambda bh, i: (bh // N_HEADS, bh % N_HEADS, i, 0)
    bh_full = lambda bh, i: (bh // N_HEADS, bh % N_HEADS, 0, 0)
    tq = _q_block(seq)
    o_c = _flash(score_bound(HEAD_DIM, lw["moba_qn_g"], lw["moba_kn_g"]), mq_aug, mbk, mbv1,
                 q_map=bh_map, kv_map=bh_full, o_map=bh_map,
                 grid=(batch * N_HEADS, seq // tq), rows=tq, tq=tq, seq=seq,
                 out_lead=(batch, N_HEADS, seq), out_dtype=BF16, window_tiles=None, name="moba_attention")

    ncmp = seq // NSA_CMP_STRIDE
    tok = lambda a: a.reshape(batch, ncmp, NSA_CMP_STRIDE * HEAD_DIM)
    pad64 = lambda a: jnp.pad(a, [(0, 0)] * (a.ndim - 1) + [(0, LANES - HEAD_DIM)])
    pe4 = lw["nsa_cmp_pe"].reshape(4, NSA_CMP_STRIDE * HEAD_DIM)
    wc = pad64(lw["nsa_cmp_w"]).astype(BF16)
    gk = pad64(lw["nsa_kn_g"][0].reshape(1, HEAD_DIM))
    kc, vc = _nsa_compress(tok(kct), tok(vct), pe4, wc[0], wc[1], gk, consts["tabs_cmp"], batch)
    q_aug, d_c = _nsa_select(nsq, kc, vc, consts["mmat"], batch, seq)

    rows = N_HEADS * t
    b_map = lambda b, i: (b, i, 0, 0)
    b_full = lambda b, i: (b, 0, 0)
    nsa_flash = functools.partial(_flash, q_map=b_map, kv_map=b_full, o_map=b_map, grid=(batch, nq),
                                  rows=rows, tq=t, seq=seq, out_lead=(batch, nq, rows), out_dtype=F32)
    d_s = nsa_flash(score_bound(HEAD_DIM, lw["nsa_qn_g"], lw["nsa_kn_g"][1]), q_aug, ksa, vs1,
                    window_tiles=None, name="nsa_selected")
    d_w = nsa_flash(score_bound(HEAD_DIM, lw["nsa_qn_g"], lw["nsa_kn_g"][2]), nsq, kwp, vw1,
                    window_tiles=NSA_WINDOW // t, name="nsa_window")
    return glog, o_a, o_b, o_c, d_c, d_s, d_w


def _layer(x2, lw, consts, batch, seq, lam_init):
    n, d = x2.shape
    glog, o_a, o_b, o_c, d_c, d_s, d_w = _mixers(x2, lw, consts, batch, seq, lam_init)
    x2 = _merge(x2, lw["attn_norm_g"].reshape(1, d), o_a, o_b, o_c, d_c, d_s, d_w,
                glog, consts["emat"], lw["w_gate"].astype(BF16), lw["b_gate"].reshape(4, 1, d),
                lw["w_branch"].astype(BF16), lw["w_out"].astype(BF16), seq, _row_tile(n, seq, 512))
    return _ffn(x2, lw["ffn_norm_g"].reshape(1, d), lw["w_ffn_gate"].astype(BF16),
                lw["w_ffn_up"].astype(BF16), lw["w_ffn_down"].astype(BF16), _row_tile(n, seq, 512))


def kernel(x, attn_norm_g, w_in, diff_qn_g, diff_kn_g, diff_lam, diff_subln_g, moba_qn_g, moba_kn_g,
           nsa_qn_g, nsa_kn_g, nsa_cmp_pe, nsa_cmp_w, w_gate, b_gate, w_branch, w_out, ffn_norm_g,
           w_ffn_gate, w_ffn_up, w_ffn_down):
    batch, seq, d = x.shape
    weights = dict(attn_norm_g=attn_norm_g, w_in=w_in, diff_qn_g=diff_qn_g, diff_kn_g=diff_kn_g,
                   diff_lam=diff_lam, diff_subln_g=diff_subln_g, moba_qn_g=moba_qn_g, moba_kn_g=moba_kn_g,
                   nsa_qn_g=nsa_qn_g, nsa_kn_g=nsa_kn_g, nsa_cmp_pe=nsa_cmp_pe, nsa_cmp_w=nsa_cmp_w,
                   w_gate=w_gate, b_gate=b_gate, w_branch=w_branch, w_out=w_out, ffn_norm_g=ffn_norm_g,
                   w_ffn_gate=w_ffn_gate, w_ffn_up=w_ffn_up, w_ffn_down=w_ffn_down)
    consts = _layer_constants(seq)
    x2 = x.reshape(batch * seq, d)
    for layer in range(w_in.shape[0]):
        lw = {k: v[layer] for k, v in weights.items()}
        lam_init = 0.8 - 0.6 * math.exp(-0.3 * layer)
        x2 = _layer(x2, lw, consts, batch, seq, lam_init)
    return x2.reshape(batch, seq, d)
```

```python
import functools
import math

import numpy as np
import jax
import jax.numpy as jnp
from jax import lax
from jax.experimental import pallas as pl
from jax.experimental.pallas import tpu as pltpu

HEAD_DIM = 64
N_HEADS = 4
MIX_WIDTH = N_HEADS * HEAD_DIM
ROPE_THETA = 500000.0
ROPE_FRACTION_DEN = 4
DIFF_DIM = HEAD_DIM // 2
MOBA_BLOCK = 256
MOBA_TOPK = 3
NSA_CMP_LEN = 32
NSA_CMP_STRIDE = 16
NSA_SLC_BLOCK = 64
NSA_TOPN = 16
NSA_WINDOW = 512
NEG_INF = -1e30
BIG = 1e30
RMS_EPS = 1e-6

LANES = 128
ATT_TILE = 256
UNROLL = 4
DIAG_ROWS = 512
GROUP = 256
N_GROUPS = 12
VMEM_LIMIT = 56 * 1024 * 1024
LOG2_E = math.log2(math.e)
EXP2_LIMIT = 80.0
NORM_SLACK = 1.05

F32 = jnp.float32
BF16 = jnp.bfloat16

_NT = (((1,), (1,)), ((), ()))


def _dot(a, b):
    return jnp.dot(a, b, preferred_element_type=F32)


def _dot_nt(a, b):
    return lax.dot_general(a, b, _NT, preferred_element_type=F32)


def _split_bf16(x):
    hi = x.astype(BF16)
    lo = (x - hi.astype(F32)).astype(BF16)
    return hi, lo


def _params(*sem):
    return pltpu.CompilerParams(dimension_semantics=sem, vmem_limit_bytes=VMEM_LIMIT)


def _resident(shape, index_map):
    return pl.BlockSpec(shape, index_map, pipeline_mode=pl.Buffered(1))


def _rope_tables(pos, d, width):
    r = d // ROPE_FRACTION_DEN
    half = r // 2
    inv = ROPE_THETA ** (-jnp.arange(half, dtype=F32) * 2.0 / r)
    ang = pos.astype(F32)[:, None] * inv[None, :]
    cos, sin = jnp.cos(ang), jnp.sin(ang)
    n = pos.shape[0]
    one = jnp.ones((n, d - r), F32)
    zero_h = jnp.zeros((n, half), F32)
    zero_t = jnp.zeros((n, d - r), F32)
    cos_g = jnp.concatenate([cos, cos, one], axis=1)
    sa_g = jnp.concatenate([-sin, zero_h, zero_t], axis=1)
    sb_g = jnp.concatenate([zero_h, sin, zero_t], axis=1)
    reps = width // d
    return (jnp.tile(cos_g, (1, reps)), jnp.tile(sa_g, (1, reps)), jnp.tile(sb_g, (1, reps)))


def _group_mean_matrix(d, width):
    g = np.arange(width) // d
    return jnp.asarray((g[:, None] == g[None, :]).astype(np.float32) / d, BF16)


def _rope_apply(y, cos, sa, sb, half):
    w = y.shape[-1]
    return y * cos + pltpu.roll(y, w - half, 1) * sa + pltpu.roll(y, half, 1) * sb


_GROUP_CFG = (
    (None, None), (None, None), (None, None),
    (DIFF_DIM, DIFF_DIM), (DIFF_DIM, DIFF_DIM), (None, None),
    (HEAD_DIM, HEAD_DIM), (HEAD_DIM, HEAD_DIM), (None, None),
    (HEAD_DIM, HEAD_DIM),
    (HEAD_DIM, HEAD_DIM),
    (HEAD_DIM, HEAD_DIM),
)


def _inproj_kernel(x_ref, g_ref, w_ref, gain_ref, nflag_ref, rflag_ref, b64_ref, b32_ref,
                   c64_ref, sa64_ref, sb64_ref, c32_ref, sa32_ref, sb32_ref, mhot_ref, shot_ref,
                   sbq, sbk, sbv, dfq, dfk, dfv, mbq, mbk, mbv, nsq, kct, vct, ksa, vs1, kwp, vw1, glog):
    x = x_ref[...]
    tm = x.shape[0]
    t = ATT_TILE
    hd = HEAD_DIM
    xn = x * lax.rsqrt(jnp.mean(x * x, axis=-1, keepdims=True) + RMS_EPS) * g_ref[...]
    xn = xn.astype(BF16)
    zeros = jnp.zeros((tm, hd), BF16)
    ones_lane = jnp.where(lax.broadcasted_iota(jnp.int32, (tm, hd), 1) == 0, 1.0, 0.0).astype(BF16)

    def project(gi):
        norm, rope = _GROUP_CFG[gi]
        y = _dot(xn, w_ref[:, gi * GROUP:(gi + 1) * GROUP])
        gain = gain_ref[gi]
        if norm is not None:
            bmat = b64_ref[...] if norm == HEAD_DIM else b32_ref[...]
            ms = _dot((y * y).astype(BF16), bmat)
            y = y * jnp.where(nflag_ref[gi] > 0, lax.rsqrt(ms + RMS_EPS) * gain, gain)
        else:
            y = y * gain
        if rope is not None:
            if rope == HEAD_DIM:
                cos, sa, sb = c64_ref[...], sa64_ref[...], sb64_ref[...]
            else:
                cos, sa, sb = c32_ref[...], sa32_ref[...], sb32_ref[...]
            rf = rflag_ref[gi]
            cos = jnp.where(rf > 0, cos, 1.0)
            y = _rope_apply(y, cos, sa * rf, sb * rf, rope // ROPE_FRACTION_DEN // 2)
        return y

    def quarters(y):
        return [y[:, h * hd:(h + 1) * hd].astype(BF16) for h in range(N_HEADS)]

    def put_heads(ref, gi, extra):
        for h, piece in enumerate(quarters(project(gi))):
            if extra is None:
                ref[h] = piece
            else:
                ref[h, :, 0:hd] = piece
                ref[h, :, hd:] = extra

    put_heads(sbq, 0, None)
    put_heads(sbk, 1, None)
    put_heads(sbv, 2, None)
    put_heads(dfq, 3, None)
    put_heads(dfk, 4, None)
    put_heads(dfv, 5, ones_lane)
    put_heads(mbq, 6, zeros)
    put_heads(mbk, 7, mhot_ref[...])
    put_heads(mbv, 8, ones_lane)

    for h, piece in enumerate(quarters(project(9))):
        for qt in range(tm // t):
            nsq[qt, h * t:(h + 1) * t, 0:hd] = piece[qt * t:(qt + 1) * t]
            nsq[qt, h * t:(h + 1) * t, hd:] = zeros[0:t]

    kc_tok, vc_tok, k_slc, v_slc = quarters(project(10))
    kct[...] = kc_tok
    vct[...] = vc_tok
    ksa[:, 0:hd] = k_slc
    ksa[:, hd:] = shot_ref[...]
    vs1[:, 0:hd] = v_slc
    vs1[:, hd:] = ones_lane

    y = project(11)
    kwp[:, 0:hd] = y[:, 0:hd].astype(BF16)
    kwp[:, hd:] = zeros
    vw1[:, 0:hd] = y[:, hd:2 * hd].astype(BF16)
    vw1[:, hd:] = ones_lane
    glog[...] = y[:, 2 * hd:].astype(BF16)


def _inproj(x2, g, w, gain, nflag, rflag, consts, batch, seq, tm):
    n, d = x2.shape
    ncols = N_GROUPS * GROUP
    nst = seq // tm
    t = ATT_TILE
    row = lambda i: (i, 0)
    fix2 = lambda i: (0, 0)
    fix3 = lambda i: (0, 0, 0)
    tab = lambda i: (i % nst, 0)
    vec = _resident((N_GROUPS, 1, GROUP), fix3)
    tspec = pl.BlockSpec((tm, GROUP), tab)
    hspec = pl.BlockSpec((tm, HEAD_DIM), tab)

    def heads_out(width):
        return (pl.BlockSpec((None, N_HEADS, tm, width), lambda i: (i // nst, 0, i % nst, 0)),
                jax.ShapeDtypeStruct((batch, N_HEADS, seq, width), BF16))

    def tokens_out(width):
        return (pl.BlockSpec((None, tm, width), lambda i: (i // nst, i % nst, 0)),
                jax.ShapeDtypeStruct((batch, seq, width), BF16))

    stacked_q = (pl.BlockSpec((None, tm // t, N_HEADS * t, LANES), lambda i: (i // nst, i % nst, 0, 0)),
                 jax.ShapeDtypeStruct((batch, seq // t, N_HEADS * t, LANES), BF16))
    outs = ([heads_out(HEAD_DIM)] * 5 + [heads_out(LANES)] * 4 + [stacked_q]
            + [tokens_out(HEAD_DIM)] * 2 + [tokens_out(LANES)] * 5)
    return pl.pallas_call(
        _inproj_kernel,
        grid=(n // tm,),
        in_specs=[pl.BlockSpec((tm, d), row), _resident((1, d), fix2), _resident((d, ncols), fix2),
                  vec, vec, vec, _resident((GROUP, GROUP), fix2), _resident((GROUP, GROUP), fix2),
                  tspec, tspec, tspec, tspec, tspec, tspec, hspec, hspec],
        out_specs=[o[0] for o in outs],
        out_shape=[o[1] for o in outs],
        compiler_params=_params("parallel"),
        name="inproj",
    )(x2, g, w, gain, nflag, rflag, _group_mean_matrix(HEAD_DIM, GROUP),
      _group_mean_matrix(DIFF_DIM, GROUP), *consts["tabs64"], *consts["tabs32"],
      consts["moba_hot"], consts["slc_hot"])


def _col_minus_row(nrows, tq=None):
    shape = (nrows, ATT_TILE)
    row = lax.broadcasted_iota(jnp.int32, shape, 0)
    if tq is not None:
        row = row & (tq - 1)
    return lax.broadcasted_iota(jnp.int32, shape, 1) - row


def _causal_sweep(step, i, rows, tq, strict):
    t = ATT_TILE
    per_block = tq // t
    base = i * per_block
    tile = lambda j: pl.multiple_of(j * t, t)
    visible = (lambda cmr, lim: cmr < lim) if strict else (lambda cmr, lim: cmr <= lim)
    if tq == t:
        step(slice(0, rows), [tile(base)], [visible(_col_minus_row(rows, tq), 0)])
    else:
        assert rows == tq
        chunk = min(DIAG_ROWS, rows)
        cmr = _col_minus_row(chunk)
        for r0 in range(0, rows, chunk):
            starts, masks = [], []
            for d in range((r0 + chunk - 1) // t, -1, -1):
                starts.append(tile(base + d))
                some_key_ahead = d * t + t - 1 > r0 - (1 if strict else 0)
                masks.append(visible(cmr, r0 - d * t) if some_key_ahead else None)
            step(slice(r0, r0 + chunk), starts, masks)

    groups = base // UNROLL

    def body(g, carry):
        newest = base - 1 - g * UNROLL
        step(slice(0, rows), [tile(newest - u) for u in range(UNROLL)], [None] * UNROLL)
        return carry

    lax.fori_loop(0, groups, body, 0)
    if per_block % UNROLL == 0:
        return
    left = base - groups * UNROLL
    for r in range(1, UNROLL):
        @pl.when(left == r)
        def _():
            step(slice(0, rows), [tile(r - 1 - u) for u in range(r)], [None] * r)


def _sb_kernel(q_ref, k_ref, v_ref, o_ref, carry_ref, acc_ref, *, tq):
    t = ATT_TILE
    i = pl.program_id(1)
    rows = q_ref.shape[0]
    r = lax.broadcasted_iota(jnp.int32, (t, t), 0)
    c = lax.broadcasted_iota(jnp.int32, (t, t), 1)
    suffix = (r > c).astype(BF16)

    def step(rs, starts, masks):
        q = q_ref[rs, :]
        carry = carry_ref[rs, :]
        pv = None
        for start, mask in zip(starts, masks):
            z = _dot_nt(q, k_ref[pl.ds(start, t), :])
            sp = jnp.maximum(z, 0.0) + jnp.log(1.0 + jnp.exp2(jnp.abs(z) * (-LOG2_E)))
            if mask is not None:
                sp = jnp.where(mask, sp, 0.0)
            later = _dot(sp.astype(BF16), suffix)
            a = jnp.exp2((z - sp - later - carry) * LOG2_E)
            if mask is not None:
                a = jnp.where(mask, a, 0.0)
            d = _dot(a.astype(BF16), v_ref[pl.ds(start, t), :])
            pv = d if pv is None else pv + d
            carry = carry + (later[:, 0:1] + sp[:, 0:1])
        acc_ref[rs, :] += pv
        carry_ref[rs, :] = carry

    carry_ref[...] = jnp.zeros_like(carry_ref)
    acc_ref[...] = jnp.zeros_like(acc_ref)
    _causal_sweep(step, i, rows, tq, True)
    o_ref[...] = acc_ref[...].astype(o_ref.dtype)


def _q_block(seq):
    return min(4 * ATT_TILE, seq)


def _head_block(rows, width):
    if rows is None:
        return lambda seq: pl.BlockSpec((None, None, seq, width),
                                        lambda bh, i: (bh // N_HEADS, bh % N_HEADS, 0, 0))
    return pl.BlockSpec((None, None, rows, width), lambda bh, i: (bh // N_HEADS, bh % N_HEADS, i, 0))


def _stick_breaking(q, k, v, batch, seq):
    tq = _q_block(seq)
    return pl.pallas_call(
        functools.partial(_sb_kernel, tq=tq),
        grid=(batch * N_HEADS, seq // tq),
        in_specs=[_head_block(tq, HEAD_DIM), _head_block(None, HEAD_DIM)(seq), _head_block(None, HEAD_DIM)(seq)],
        out_specs=_head_block(tq, HEAD_DIM),
        out_shape=jax.ShapeDtypeStruct((batch, N_HEADS, seq, HEAD_DIM), BF16),
        scratch_shapes=[pltpu.VMEM((tq, 1), F32), pltpu.VMEM((tq, HEAD_DIM), F32)],
        compiler_params=_params("parallel", "arbitrary"),
        name="stick_breaking",
    )(q, k, v)


def _softmax_step(q, k_ref, v_ref, starts, masks, rs, m_ref, acc_ref):
    t = ATT_TILE
    pv = None
    for start, mask in reversed(list(zip(starts, masks))):
        s = _dot_nt(q, k_ref[pl.ds(start, t), :])
        if mask is not None:
            s = jnp.where(mask, s, NEG_INF)
        v1 = v_ref[pl.ds(start, t), :]
        if m_ref is None:
            d = _dot(jnp.exp2(s).astype(BF16), v1)
            pv = d if pv is None else pv + d
            continue
        m_old = m_ref[rs, :]
        mn = jnp.maximum(m_old, jnp.max(s, axis=1, keepdims=True))
        acc_ref[rs, :] = jnp.exp2(m_old - mn) * acc_ref[rs, :] + _dot(jnp.exp2(s - mn).astype(BF16), v1)
        m_ref[rs, :] = mn
    if m_ref is None:
        acc_ref[rs, :] += pv


def _either_path(bound_ref, m_refs, sweep):
    bounded = bound_ref[0] <= EXP2_LIMIT

    @pl.when(bounded)
    def _():
        sweep((None,) * len(m_refs))

    @pl.when(jnp.logical_not(bounded))
    def _():
        for ref in m_refs:
            ref[...] = jnp.full(ref.shape, NEG_INF, F32)
        sweep(m_refs)


def _normalised(acc):
    return acc[:, :HEAD_DIM] / acc[:, HEAD_DIM:HEAD_DIM + 1]


def _diff_kernel(bound_ref, q_ref, k_ref, v_ref, lam_ref, g_ref, o_ref, q1_ref, q2_ref, m1_ref, m2_ref,
                 a1_ref, a2_ref, *, lam_init, tq):
    i = pl.program_id(1)
    rows = q_ref.shape[0]
    q = q_ref[...]
    lane = lax.broadcasted_iota(jnp.int32, q.shape, 1)
    zero = jnp.zeros_like(q)
    q1_ref[...] = jnp.where(lane < DIFF_DIM, q, zero)
    q2_ref[...] = jnp.where(lane >= DIFF_DIM, q, zero)
    for ref in (a1_ref, a2_ref):
        ref[...] = jnp.zeros_like(ref)

    def sweep(m_refs):
        def step(rs, starts, mask):
            _softmax_step(q1_ref[rs, :], k_ref, v_ref, starts, mask, rs, m_refs[0], a1_ref)
            _softmax_step(q2_ref[rs, :], k_ref, v_ref, starts, mask, rs, m_refs[1], a2_ref)

        _causal_sweep(step, i, rows, tq, False)

    _either_path(bound_ref, (m1_ref, m2_ref), sweep)

    lp = lam_ref[...]
    lam = (jnp.exp(jnp.sum(lp[0:1, :] * lp[1:2, :], axis=1, keepdims=True))
           - jnp.exp(jnp.sum(lp[2:3, :] * lp[3:4, :], axis=1, keepdims=True)) + lam_init)
    o = _normalised(a1_ref[...]) - lam * _normalised(a2_ref[...])
    o = o * lax.rsqrt(jnp.mean(o * o, axis=-1, keepdims=True) + RMS_EPS) * g_ref[...]
    o_ref[...] = (o * (1.0 - lam_init)).astype(o_ref.dtype)


_SCALAR = pl.BlockSpec(memory_space=pltpu.SMEM)


def _diff_attention(bound, q, k, v1, lam_p, subln_g, batch, seq, lam_init):
    tq = _q_block(seq)
    fix = lambda bh, i: (0, 0)
    return pl.pallas_call(
        functools.partial(_diff_kernel, lam_init=lam_init, tq=tq),
        grid=(batch * N_HEADS, seq // tq),
        in_specs=[_SCALAR, _head_block(tq, HEAD_DIM), _head_block(None, HEAD_DIM)(seq),
                  _head_block(None, LANES)(seq),
                  pl.BlockSpec((4, DIFF_DIM), fix), pl.BlockSpec((1, HEAD_DIM), fix)],
        out_specs=_head_block(tq, HEAD_DIM),
        out_shape=jax.ShapeDtypeStruct((batch, N_HEADS, seq, HEAD_DIM), BF16),
        scratch_shapes=[pltpu.VMEM((tq, HEAD_DIM), BF16), pltpu.VMEM((tq, HEAD_DIM), BF16),
                        pltpu.VMEM((tq, 1), F32), pltpu.VMEM((tq, 1), F32),
                        pltpu.VMEM((tq, LANES), F32), pltpu.VMEM((tq, LANES), F32)],
        compiler_params=_params("parallel", "arbitrary"),
        name="diff_attention",
    )(bound, q, k, v1, lam_p, subln_g)


def _select_bias(score_t, keep_extra_t, ntop):
    nblk = score_t.shape[0]
    sub = lax.broadcasted_iota(jnp.int32, score_t.shape, 0)
    rank = jnp.zeros(score_t.shape, jnp.int32)
    for b in range(nblk):
        row = score_t[b:b + 1, :]
        ahead = (row > score_t) | ((row == score_t) & (sub > b))
        rank = rank + ahead.astype(jnp.int32)
    keep = rank < ntop
    if keep_extra_t is not None:
        keep = keep_extra_t(keep, sub)
    return jnp.where(keep, 0.0, NEG_INF)


def _bias_lanes(bias_t, lane0):
    nblk, t = bias_t.shape
    parts = []
    if lane0:
        parts.append(jnp.zeros((lane0, t), F32))
    parts.append(bias_t)
    if LANES - lane0 - nblk:
        parts.append(jnp.zeros((LANES - lane0 - nblk, t), F32))
    return jnp.concatenate(parts, axis=0).T


def _moba_gate_kernel(q_ref, k_ref, a_ref, o_ref, km_hi, km_lo, *, nblk):
    i = pl.program_id(1)

    @pl.when(i == 0)
    def _():
        kmean = _dot(a_ref[...], k_ref[...])
        hi, lo = _split_bf16(kmean)
        km_hi[...] = hi
        km_lo[...] = lo

    q = q_ref[...]
    rows = q.shape[0]
    gate = _dot_nt(q, km_hi[...]) + _dot_nt(q, km_lo[...])
    gate_t = gate.T[HEAD_DIM:HEAD_DIM + nblk, :]
    blk = lax.broadcasted_iota(jnp.int32, gate_t.shape, 0)
    own = (i * rows + lax.broadcasted_iota(jnp.int32, gate_t.shape, 1)) // MOBA_BLOCK
    score_t = jnp.where(blk < own, gate_t, NEG_INF)
    ksel = min(MOBA_TOPK, nblk - 1)

    def keep_rule(keep, sub):
        return (keep & (sub < own)) | (sub == own)

    bias_t = _select_bias(score_t, keep_rule, ksel)
    o_ref[...] = (q.astype(F32) + _bias_lanes(bias_t, HEAD_DIM)).astype(o_ref.dtype)


def _moba_gate(q_pad, k_aug, amat, batch, seq):
    tq = _q_block(seq)
    nblk = seq // MOBA_BLOCK
    return pl.pallas_call(
        functools.partial(_moba_gate_kernel, nblk=nblk),
        grid=(batch * N_HEADS, seq // tq),
        in_specs=[_head_block(tq, LANES), _head_block(None, LANES)(seq),
                  pl.BlockSpec((LANES, seq), lambda bh, i: (0, 0))],
        out_specs=_head_block(tq, LANES),
        out_shape=jax.ShapeDtypeStruct((batch, N_HEADS, seq, LANES), BF16),
        scratch_shapes=[pltpu.VMEM((LANES, LANES), BF16), pltpu.VMEM((LANES, LANES), BF16)],
        compiler_params=_params("parallel", "arbitrary"),
        name="moba_gate",
    )(q_pad, k_aug, amat)


def _flash_kernel(bound_ref, q_ref, k_ref, v_ref, o_ref, m_ref, acc_ref, *, tq, window_tiles):
    t = ATT_TILE
    i = pl.program_id(1)
    rows = q_ref.shape[0]
    acc_ref[...] = jnp.zeros_like(acc_ref)

    def sweep(m_refs):
        def step(rs, starts, mask):
            _softmax_step(q_ref[rs, :], k_ref, v_ref, starts, mask, rs, m_refs[0], acc_ref)

        if window_tiles is None:
            _causal_sweep(step, i, rows, tq, False)
            return
        assert tq == t
        everything = slice(0, rows)
        tile = lambda back: pl.multiple_of((i - back) * t, t)
        cmr = _col_minus_row(rows, tq)

        @pl.when(i >= window_tiles)
        def _():
            step(everything, [tile(b) for b in range(window_tiles + 1)],
                 [cmr <= 0] + [None] * (window_tiles - 1) + [cmr > 0])

        @pl.when(i < window_tiles)
        def _():
            step(everything, [tile(0)], [cmr <= 0])
            for back in range(1, window_tiles):
                @pl.when(i >= back)
                def _():
                    step(everything, [tile(back)], [None])

    _either_path(bound_ref, (m_ref,), sweep)
    o_ref[...] = _normalised(acc_ref[...]).astype(o_ref.dtype)


def _flash(bound, q, k, v1, *, q_map, kv_map, o_map, grid, rows, tq, seq, out_lead, out_dtype, window_tiles,
           name):
    lead = (None,) * (q.ndim - 2)
    klead = (None,) * (k.ndim - 2)
    return pl.pallas_call(
        functools.partial(_flash_kernel, tq=tq, window_tiles=window_tiles),
        grid=grid,
        in_specs=[_SCALAR, pl.BlockSpec(lead + (rows, LANES), q_map),
                  pl.BlockSpec(klead + (seq, LANES), kv_map),
                  pl.BlockSpec(klead + (seq, LANES), kv_map)],
        out_specs=pl.BlockSpec(lead + (rows, HEAD_DIM), o_map),
        out_shape=jax.ShapeDtypeStruct(out_lead + (HEAD_DIM,), out_dtype),
        scratch_shapes=[pltpu.VMEM((rows, 1), F32), pltpu.VMEM((rows, LANES), F32)],
        compiler_params=_params("parallel", "arbitrary"),
        name=name,
    )(bound, q, k, v1)


def _nsa_compress_kernel(tk_ref, tv_ref, pe_ref, wk_ref, wv_ref, g_ref, cos_ref, sa_ref, sb_ref,
                         kc_ref, vc_ref):
    rows = tk_ref.shape[0]

    def compress(tok, pe_lo, pe_hi, w_ref):
        half = w_ref.shape[0] // 2
        a = _dot((tok + pe_lo).astype(BF16), w_ref[0:half, :])
        b = _dot((tok + pe_hi).astype(BF16), w_ref[half:, :])
        return a + pltpu.roll(b, rows - 1, 0)

    kc = compress(tk_ref[...].astype(F32), pe_ref[0:1, :], pe_ref[1:2, :], wk_ref)
    vc = compress(tv_ref[...].astype(F32), pe_ref[2:3, :], pe_ref[3:4, :], wv_ref)
    ms = jnp.sum(kc * kc, axis=-1, keepdims=True) / HEAD_DIM
    kc = kc * lax.rsqrt(ms + RMS_EPS) * g_ref[...]
    kc = _rope_apply(kc, cos_ref[...], sa_ref[...], sb_ref[...], HEAD_DIM // ROPE_FRACTION_DEN // 2)
    kc_ref[...] = kc.astype(kc_ref.dtype)
    vc_ref[...] = vc.astype(vc_ref.dtype)


def _nsa_compress(tk, tv, pe4, wk, wv, gk, tabs, batch):
    rows, width = tk.shape[1], tk.shape[2]
    b3 = lambda b: (b, 0, 0)
    fix = lambda b: (0, 0)
    tok = pl.BlockSpec((None, rows, width), b3)
    out = pl.BlockSpec((None, rows, LANES), b3)
    tab = pl.BlockSpec((rows, LANES), fix)
    return pl.pallas_call(
        _nsa_compress_kernel,
        grid=(batch,),
        in_specs=[tok, tok, pl.BlockSpec((4, width), fix),
                  pl.BlockSpec((2 * width, LANES), fix), pl.BlockSpec((2 * width, LANES), fix),
                  pl.BlockSpec((1, LANES), fix), tab, tab, tab],
        out_specs=[out, out],
        out_shape=[jax.ShapeDtypeStruct((batch, rows, LANES), BF16)] * 2,
        compiler_params=_params("parallel"),
        name="nsa_compress",
    )(tk, tv, pe4, wk, wv, gk, *tabs)


def _nsa_select_kernel(q_ref, kc_ref, vc_ref, m_ref, qa_ref, oc_ref, *, nslc, ntop):
    t = ATT_TILE
    i = pl.program_id(1)
    q = q_ref[...]
    rows = q.shape[0]
    ncmp = kc_ref.shape[0]
    s = _dot_nt(q, kc_ref[...])
    qpos = i * t + (lax.broadcasted_iota(jnp.int32, (rows, ncmp), 0) & (t - 1))
    cmp_end = NSA_CMP_STRIDE * lax.broadcasted_iota(jnp.int32, (rows, ncmp), 1) + (NSA_CMP_LEN - 1)
    mask = cmp_end <= qpos
    s = jnp.where(mask, s, NEG_INF)
    e = jnp.where(mask, jnp.exp2(s - jnp.max(s, axis=1, keepdims=True)), 0.0)
    p = e / jnp.maximum(jnp.sum(e, axis=1, keepdims=True), 1e-30)
    oc_ref[...] = _dot(p.astype(BF16), vc_ref[...])[:, :HEAD_DIM].astype(oc_ref.dtype)

    psum = p[0:t]
    for h in range(1, rows // t):
        psum = psum + p[h * t:(h + 1) * t]
    hi, lo = _split_bf16(psum)
    imp = _dot(hi, m_ref[...]) + _dot(lo, m_ref[...])
    imp_t = imp.T[HEAD_DIM:HEAD_DIM + nslc, :]
    blk = lax.broadcasted_iota(jnp.int32, imp_t.shape, 0)
    cur = (i * t + lax.broadcasted_iota(jnp.int32, imp_t.shape, 1)) // NSA_SLC_BLOCK
    forced = (blk == 0) | (blk == cur) | (blk == cur - 1)
    allowed = blk <= cur
    score_t = jnp.where(allowed, jnp.where(forced, BIG, imp_t), NEG_INF)
    bias = _bias_lanes(_select_bias(score_t, None, ntop), HEAD_DIM)
    qf = q.astype(F32)
    for h in range(rows // t):
        qa_ref[h * t:(h + 1) * t, :] = (qf[h * t:(h + 1) * t] + bias).astype(qa_ref.dtype)


def _nsa_select(q4, kc, vc, mmat, batch, seq):
    t = ATT_TILE
    rows = N_HEADS * t
    ncmp = kc.shape[1]
    nslc = seq // NSA_SLC_BLOCK
    qmap = lambda b, i: (b, i, 0, 0)
    kmap = lambda b, i: (b, 0, 0)
    return pl.pallas_call(
        functools.partial(_nsa_select_kernel, nslc=nslc, ntop=min(NSA_TOPN, nslc)),
        grid=(batch, seq // t),
        in_specs=[pl.BlockSpec((None, None, rows, LANES), qmap),
                  pl.BlockSpec((None, ncmp, LANES), kmap), pl.BlockSpec((None, ncmp, LANES), kmap),
                  pl.BlockSpec((ncmp, LANES), lambda b, i: (0, 0))],
        out_specs=[pl.BlockSpec((None, None, rows, LANES), qmap),
                   pl.BlockSpec((None, None, rows, HEAD_DIM), qmap)],
        out_shape=[jax.ShapeDtypeStruct((batch, seq // t, rows, LANES), BF16),
                   jax.ShapeDtypeStruct((batch, seq // t, rows, HEAD_DIM), F32)],
        compiler_params=_params("parallel", "arbitrary"),
        name="nsa_select",
    )(q4, kc, vc, mmat)


def _merge_kernel(x_ref, g_ref, oa_ref, ob_ref, oc_ref, dc_ref, ds_ref, dw_ref, gl_ref, e_ref,
                  wg_ref, bg_ref, wb_ref, wo_ref, y_ref):
    t = ATT_TILE
    x = x_ref[...]
    xn = x * lax.rsqrt(jnp.mean(x * x, axis=-1, keepdims=True) + RMS_EPS) * g_ref[...]
    xn = xn.astype(BF16)

    def token_major(ref):
        return jnp.concatenate([ref[h] for h in range(N_HEADS)], axis=1)

    def unstack(ref):
        return jnp.concatenate(
            [jnp.concatenate([ref[qt, h * t:(h + 1) * t, :] for h in range(N_HEADS)], axis=1)
             for qt in range(ref.shape[0])], axis=0)

    sig = 1.0 / (1.0 + jnp.exp(-gl_ref[...].astype(F32)))
    hi, lo = _split_bf16(sig)
    gexp = _dot(hi, e_ref[...]) + _dot(lo, e_ref[...])
    w = MIX_WIDTH
    o_d = (gexp[:, 0:w] * unstack(dc_ref) + gexp[:, w:2 * w] * unstack(ds_ref)
           + gexp[:, 2 * w:3 * w] * unstack(dw_ref)).astype(BF16)
    merged = None
    for bi, o in enumerate((token_major(oa_ref), token_major(ob_ref), token_major(oc_ref), o_d)):
        gate = 1.0 / (1.0 + jnp.exp(-(_dot(xn, wg_ref[bi]) + bg_ref[bi])))
        term = gate * _dot(o, wb_ref[bi])
        merged = term if merged is None else merged + term
    y_ref[...] = x + _dot(merged.astype(BF16), wo_ref[...])


def _merge(x2, g, oa, ob, oc, dc, ds, dw, glog, emat, wg, bg, wb, wo, seq, tm):
    n, d = x2.shape
    nst = seq // tm
    t = ATT_TILE
    row = lambda i: (i, 0)
    fix2 = lambda i: (0, 0)
    fix3 = lambda i: (0, 0, 0)
    heads = pl.BlockSpec((None, N_HEADS, tm, HEAD_DIM), lambda i: (i // nst, 0, i % nst, 0))
    stacked = pl.BlockSpec((None, tm // t, N_HEADS * t, HEAD_DIM), lambda i: (i // nst, i % nst, 0, 0))
    return pl.pallas_call(
        _merge_kernel,
        grid=(n // tm,),
        in_specs=[pl.BlockSpec((tm, d), row), _resident((1, d), fix2), heads, heads, heads,
                  stacked, stacked, stacked,
                  pl.BlockSpec((None, tm, LANES), lambda i: (i // nst, i % nst, 0)),
                  _resident((LANES, 3 * MIX_WIDTH), fix2),
                  _resident((4, d, d), fix3), _resident((4, 1, d), fix3),
                  _resident((4, MIX_WIDTH, d), fix3), _resident((d, d), fix2)],
        out_specs=pl.BlockSpec((tm, d), row),
        out_shape=jax.ShapeDtypeStruct((n, d), F32),
        compiler_params=_params("parallel"),
        name="gated_merge",
    )(x2, g, oa, ob, oc, dc, ds, dw, glog, emat, wg, bg, wb, wo)


def _ffn_kernel(x_ref, g_ref, wg_ref, wu_ref, wd_ref, y_ref):
    x = x_ref[...]
    hn = x * lax.rsqrt(jnp.mean(x * x, axis=-1, keepdims=True) + RMS_EPS) * g_ref[...]
    hn = hn.astype(BF16)
    gate = _dot(hn, wg_ref[...])
    up = _dot(hn, wu_ref[...])
    act = gate * (1.0 / (1.0 + jnp.exp(-gate))) * up
    y_ref[...] = x + _dot(act.astype(BF16), wd_ref[...])


def _ffn(x2, g, wg, wu, wd, tm):
    n, d = x2.shape
    f = wg.shape[1]
    row = lambda i: (i, 0)
    fix = lambda i: (0, 0)
    return pl.pallas_call(
        _ffn_kernel,
        grid=(n // tm,),
        in_specs=[pl.BlockSpec((tm, d), row), _resident((1, d), fix), _resident((d, f), fix),
                  _resident((d, f), fix), _resident((f, d), fix)],
        out_specs=pl.BlockSpec((tm, d), row),
        out_shape=jax.ShapeDtypeStruct((n, d), F32),
        compiler_params=_params("parallel"),
        name="swiglu",
    )(x2, g, wg, wu, wd)


def _layer_constants(seq):
    t = ATT_TILE
    pos = jnp.arange(seq, dtype=jnp.int32)
    tabs64 = _rope_tables(pos, HEAD_DIM, GROUP)
    tabs32 = _rope_tables(pos, DIFF_DIM, GROUP)
    ncmp = seq // NSA_CMP_STRIDE
    cmp_end = NSA_CMP_STRIDE * jnp.arange(ncmp, dtype=jnp.int32) + (NSA_CMP_LEN - 1)
    tabs_cmp = _rope_tables(cmp_end, HEAD_DIM, HEAD_DIM)
    tabs_cmp = tuple(jnp.pad(a, ((0, 0), (0, LANES - HEAD_DIM)), constant_values=c)
                     for a, c in zip(tabs_cmp, (1.0, 0.0, 0.0)))

    s = np.arange(seq)
    nblk = seq // MOBA_BLOCK
    nslc = seq // NSA_SLC_BLOCK
    assert nblk <= LANES - HEAD_DIM and nslc <= LANES - HEAD_DIM and seq % t == 0
    moba_hot = np.zeros((seq, LANES - HEAD_DIM), np.float32)
    moba_hot[s, s // MOBA_BLOCK] = 1.0
    slc_hot = np.zeros((seq, LANES - HEAD_DIM), np.float32)
    slc_hot[s, s // NSA_SLC_BLOCK] = 1.0
    amat = np.zeros((LANES, seq), np.float32)
    amat[HEAD_DIM + s // MOBA_BLOCK, s] = 1.0 / MOBA_BLOCK

    nc_real = (seq - NSA_CMP_LEN) // NSA_CMP_STRIDE + 1
    cstart = NSA_CMP_STRIDE * np.arange(ncmp)
    sstart = NSA_SLC_BLOCK * np.arange(nslc)
    overlap = np.clip(np.minimum(cstart[:, None] + NSA_CMP_LEN, sstart[None, :] + NSA_SLC_BLOCK)
                      - np.maximum(cstart[:, None], sstart[None, :]), 0, None)
    mmat = np.zeros((ncmp, LANES), np.float32)
    mmat[:nc_real, HEAD_DIM:HEAD_DIM + nslc] = overlap[:nc_real].astype(np.float32) / NSA_CMP_STRIDE

    emat = np.zeros((LANES, 3 * MIX_WIDTH), np.float32)
    for br in range(3):
        for hd in range(N_HEADS):
            emat[br * N_HEADS + hd,
                 br * MIX_WIDTH + hd * HEAD_DIM: br * MIX_WIDTH + (hd + 1) * HEAD_DIM] = 1.0

    return dict(tabs64=tabs64, tabs32=tabs32, tabs_cmp=tabs_cmp,
                moba_hot=jnp.asarray(moba_hot, BF16), slc_hot=jnp.asarray(slc_hot, BF16),
                amat=jnp.asarray(amat, BF16), mmat=jnp.asarray(mmat, BF16), emat=jnp.asarray(emat, BF16))


def _lane_vectors(diff_qn_g, diff_kn_g, moba_qn_g, moba_kn_g, nsa_qn_g, nsa_kn_g):
    ones = jnp.ones((GROUP,), F32)
    zeros = jnp.zeros((GROUP,), F32)
    sc64_log2 = LOG2_E / math.sqrt(HEAD_DIM)
    sc32_log2 = LOG2_E / math.sqrt(DIFF_DIM)
    quarter = lambda g, k: jnp.concatenate([g if j == k else jnp.ones((HEAD_DIM,), F32) for j in range(4)])
    flag = lambda k: jnp.concatenate([jnp.full((HEAD_DIM,), 1.0 if j == k else 0.0, F32) for j in range(4)])
    gains = [ones / math.sqrt(HEAD_DIM), ones, ones,
             jnp.tile(diff_qn_g, GROUP // DIFF_DIM) * sc32_log2, jnp.tile(diff_kn_g, GROUP // DIFF_DIM), ones,
             jnp.tile(moba_qn_g, N_HEADS) * sc64_log2, jnp.tile(moba_kn_g, N_HEADS), ones,
             jnp.tile(nsa_qn_g, N_HEADS) * sc64_log2,
             quarter(nsa_kn_g[1], 2), quarter(nsa_kn_g[2], 0)]
    flags = [zeros, zeros, zeros, ones, ones, zeros, ones, ones, zeros, ones, flag(2), flag(0)]
    shape = (N_GROUPS, 1, GROUP)
    return jnp.stack(gains).reshape(shape), jnp.stack(flags).reshape(shape), jnp.stack(flags).reshape(shape)


def _row_tile(n, seq, want):
    tm = want if (n % want == 0 and seq % want == 0) else ATT_TILE
    return min(tm, seq)


def _mixers(x2, lw, consts, batch, seq, lam_init):
    n, d = x2.shape
    t = ATT_TILE
    nq = seq // t
    tm = _row_tile(n, seq, 512)

    gain, nflag, rflag = _lane_vectors(lw["diff_qn_g"], lw["diff_kn_g"], lw["moba_qn_g"],
                                       lw["moba_kn_g"], lw["nsa_qn_g"], lw["nsa_kn_g"])
    w_in = jnp.pad(lw["w_in"], ((0, 0), (0, N_GROUPS * GROUP - lw["w_in"].shape[1]))).astype(BF16)
    (sbq, sbk, sbv, dfq, dfk, dfv1, mbq, mbk, mbv1, nsq, kct, vct, ksa, vs1, kwp, vw1, glog) = _inproj(
        x2, lw["attn_norm_g"].reshape(1, d), w_in, gain, nflag, rflag, consts, batch, seq, tm)

    amax = lambda g: jnp.max(jnp.abs(g))
    score_bound = lambda d_head, gq, gk: (math.sqrt(d_head) * LOG2_E * NORM_SLACK * NORM_SLACK
                                          * amax(gq) * amax(gk)).reshape(1).astype(F32)

    o_a = _stick_breaking(sbq, sbk, sbv, batch, seq)
    o_b = _diff_attention(score_bound(DIFF_DIM, lw["diff_qn_g"], lw["diff_kn_g"]), dfq, dfk, dfv1,
                          lw["diff_lam"], lw["diff_subln_g"].reshape(1, HEAD_DIM), batch, seq, lam_init)

    mq_aug = _moba_gate(mbq, mbk, consts["amat"], batch, seq)
    bh_map = lambda bh, i: (bh // N_HEADS, bh % N_HEADS, i, 0)
    bh_full = lambda bh, i: (bh // N_HEADS, bh % N_HEADS, 0, 0)
    tq = _q_block(seq)
    o_c = _flash(score_bound(HEAD_DIM, lw["moba_qn_g"], lw["moba_kn_g"]), mq_aug, mbk, mbv1,
                 q_map=bh_map, kv_map=bh_full, o_map=bh_map,
                 grid=(batch * N_HEADS, seq // tq), rows=tq, tq=tq, seq=seq,
                 out_lead=(batch, N_HEADS, seq), out_dtype=BF16, window_tiles=None, name="moba_attention")

    ncmp = seq // NSA_CMP_STRIDE
    tok = lambda a: a.reshape(batch, ncmp, NSA_CMP_STRIDE * HEAD_DIM)
    pad64 = lambda a: jnp.pad(a, [(0, 0)] * (a.ndim - 1) + [(0, LANES - HEAD_DIM)])
    pe4 = lw["nsa_cmp_pe"].reshape(4, NSA_CMP_STRIDE * HEAD_DIM)
    wc = pad64(lw["nsa_cmp_w"]).astype(BF16)
    gk = pad64(lw["nsa_kn_g"][0].reshape(1, HEAD_DIM))
    kc, vc = _nsa_compress(tok(kct), tok(vct), pe4, wc[0], wc[1], gk, consts["tabs_cmp"], batch)
    q_aug, d_c = _nsa_select(nsq, kc, vc, consts["mmat"], batch, seq)

    rows = N_HEADS * t
    b_map = lambda b, i: (b, i, 0, 0)
    b_full = lambda b, i: (b, 0, 0)
    nsa_flash = functools.partial(_flash, q_map=b_map, kv_map=b_full, o_map=b_map, grid=(batch, nq),
                                  rows=rows, tq=t, seq=seq, out_lead=(batch, nq, rows), out_dtype=F32)
    d_s = nsa_flash(score_bound(HEAD_DIM, lw["nsa_qn_g"], lw["nsa_kn_g"][1]), q_aug, ksa, vs1,
                    window_tiles=None, name="nsa_selected")
    d_w = nsa_flash(score_bound(HEAD_DIM, lw["nsa_qn_g"], lw["nsa_kn_g"][2]), nsq, kwp, vw1,
                    window_tiles=NSA_WINDOW // t, name="nsa_window")
    return glog, o_a, o_b, o_c, d_c, d_s, d_w


def _layer(x2, lw, consts, batch, seq, lam_init):
    n, d = x2.shape
    glog, o_a, o_b, o_c, d_c, d_s, d_w = _mixers(x2, lw, consts, batch, seq, lam_init)
    x2 = _merge(x2, lw["attn_norm_g"].reshape(1, d), o_a, o_b, o_c, d_c, d_s, d_w,
                glog, consts["emat"], lw["w_gate"].astype(BF16), lw["b_gate"].reshape(4, 1, d),
                lw["w_branch"].astype(BF16), lw["w_out"].astype(BF16), seq, _row_tile(n, seq, 512))
    return _ffn(x2, lw["ffn_norm_g"].reshape(1, d), lw["w_ffn_gate"].astype(BF16),
                lw["w_ffn_up"].astype(BF16), lw["w_ffn_down"].astype(BF16), _row_tile(n, seq, 256))


def kernel(x, attn_norm_g, w_in, diff_qn_g, diff_kn_g, diff_lam, diff_subln_g, moba_qn_g, moba_kn_g,
           nsa_qn_g, nsa_kn_g, nsa_cmp_pe, nsa_cmp_w, w_gate, b_gate, w_branch, w_out, ffn_norm_g,
           w_ffn_gate, w_ffn_up, w_ffn_down):
    batch, seq, d = x.shape
    weights = dict(attn_norm_g=attn_norm_g, w_in=w_in, diff_qn_g=diff_qn_g, diff_kn_g=diff_kn_g,
                   diff_lam=diff_lam, diff_subln_g=diff_subln_g, moba_qn_g=moba_qn_g, moba_kn_g=moba_kn_g,
                   nsa_qn_g=nsa_qn_g, nsa_kn_g=nsa_kn_g, nsa_cmp_pe=nsa_cmp_pe, nsa_cmp_w=nsa_cmp_w,
                   w_gate=w_gate, b_gate=b_gate, w_branch=w_branch, w_out=w_out, ffn_norm_g=ffn_norm_g,
                   w_ffn_gate=w_ffn_gate, w_ffn_up=w_ffn_up, w_ffn_down=w_ffn_down)
    consts = _layer_constants(seq)
    x2 = x.reshape(batch * seq, d)
    for layer in range(w_in.shape[0]):
        lw = {k: v[layer] for k, v in weights.items()}
        lam_init = 0.8 - 0.6 * math.exp(-0.3 * layer)
        x2 = _layer(x2, lw, consts, batch, seq, lam_init)
    return x2.reshape(batch, seq, d)
```

```python
import functools
import math
from typing import Any, NamedTuple

import numpy as np
import jax
import jax.numpy as jnp
from jax import lax
from jax.experimental import pallas as pl
from jax.experimental.pallas import tpu as pltpu

HEAD_DIM = 64
N_HEADS = 4
MIX_WIDTH = N_HEADS * HEAD_DIM
ROPE_THETA = 500000.0
ROPE_FRACTION_DEN = 4
DIFF_DIM = HEAD_DIM // 2
MOBA_BLOCK = 256
MOBA_TOPK = 3
NSA_CMP_LEN = 32
NSA_CMP_STRIDE = 16
NSA_SLC_BLOCK = 64
NSA_TOPN = 16
NSA_WINDOW = 512
NEG_INF = -1e30
BIG = 1e30
RMS_EPS = 1e-6

LANES = 128
ATT_TILE = 256
UNROLL = 4
DIAG_ROWS = 512
GROUP = 256
N_GROUPS = 12
VMEM_LIMIT = 56 * 1024 * 1024
LOG2_E = math.log2(math.e)
EXP2_LIMIT = 80.0
NORM_SLACK = 1.05

F32 = jnp.float32
BF16 = jnp.bfloat16

_NT = (((1,), (1,)), ((), ()))


def _dot(a, b):
    return jnp.dot(a, b, preferred_element_type=F32)


def _dot_nt(a, b):
    return lax.dot_general(a, b, _NT, preferred_element_type=F32)


def _split_bf16(x):
    hi = x.astype(BF16)
    lo = (x - hi.astype(F32)).astype(BF16)
    return hi, lo


def _params(*sem):
    return pltpu.CompilerParams(dimension_semantics=sem, vmem_limit_bytes=VMEM_LIMIT)


def _resident(shape, index_map):
    return pl.BlockSpec(shape, index_map, pipeline_mode=pl.Buffered(1))


def _rope_tables(pos, d, width):
    r = d // ROPE_FRACTION_DEN
    half = r // 2
    inv = ROPE_THETA ** (-jnp.arange(half, dtype=F32) * 2.0 / r)
    ang = pos.astype(F32)[:, None] * inv[None, :]
    cos, sin = jnp.cos(ang), jnp.sin(ang)
    n = pos.shape[0]
    one = jnp.ones((n, d - r), F32)
    zero_h = jnp.zeros((n, half), F32)
    zero_t = jnp.zeros((n, d - r), F32)
    cos_g = jnp.concatenate([cos, cos, one], axis=1)
    sa_g = jnp.concatenate([-sin, zero_h, zero_t], axis=1)
    sb_g = jnp.concatenate([zero_h, sin, zero_t], axis=1)
    reps = width // d
    return (jnp.tile(cos_g, (1, reps)), jnp.tile(sa_g, (1, reps)), jnp.tile(sb_g, (1, reps)))


def _group_mean_matrix(d, width):
    g = np.arange(width) // d
    return jnp.asarray((g[:, None] == g[None, :]).astype(np.float32) / d, BF16)


def _rope_apply(y, cos, sa, sb, half):
    w = y.shape[-1]
    return y * cos + pltpu.roll(y, w - half, 1) * sa + pltpu.roll(y, half, 1) * sb


_GROUP_CFG = (
    (None, None), (None, None), (None, None),
    (DIFF_DIM, DIFF_DIM), (DIFF_DIM, DIFF_DIM), (None, None),
    (HEAD_DIM, HEAD_DIM), (HEAD_DIM, HEAD_DIM), (None, None),
    (HEAD_DIM, HEAD_DIM),
    (HEAD_DIM, HEAD_DIM),
    (HEAD_DIM, HEAD_DIM),
)


def _inproj_kernel(x_ref, g_ref, w_ref, gain_ref, nflag_ref, rflag_ref, b64_ref, b32_ref,
                   c64_ref, sa64_ref, sb64_ref, c32_ref, sa32_ref, sb32_ref, mhot_ref, shot_ref,
                   sbq, sbk, sbv, dfq, dfk, dfv, mbq, mbk, mbv, nsq, kct, vct, ksa, vs1, kwp, vw1, glog):
    x = x_ref[...]
    tm = x.shape[0]
    t = ATT_TILE
    hd = HEAD_DIM
    xn = x * lax.rsqrt(jnp.mean(x * x, axis=-1, keepdims=True) + RMS_EPS) * g_ref[...]
    xn = xn.astype(BF16)
    zeros = jnp.zeros((tm, hd), BF16)
    ones_lane = jnp.where(lax.broadcasted_iota(jnp.int32, (tm, hd), 1) == 0, 1.0, 0.0).astype(BF16)

    def project(gi):
        norm, rope = _GROUP_CFG[gi]
        y = _dot(xn, w_ref[:, gi * GROUP:(gi + 1) * GROUP])
        gain = gain_ref[gi]
        if norm is not None:
            bmat = b64_ref[...] if norm == HEAD_DIM else b32_ref[...]
            ms = _dot((y * y).astype(BF16), bmat)
            y = y * jnp.where(nflag_ref[gi] > 0, lax.rsqrt(ms + RMS_EPS) * gain, gain)
        else:
            y = y * gain
        if rope is not None:
            if rope == HEAD_DIM:
                cos, sa, sb = c64_ref[...], sa64_ref[...], sb64_ref[...]
            else:
                cos, sa, sb = c32_ref[...], sa32_ref[...], sb32_ref[...]
            rf = rflag_ref[gi]
            cos = jnp.where(rf > 0, cos, 1.0)
            y = _rope_apply(y, cos, sa * rf, sb * rf, rope // ROPE_FRACTION_DEN // 2)
        return y

    def quarters(y):
        return [y[:, h * hd:(h + 1) * hd].astype(BF16) for h in range(N_HEADS)]

    def put_heads(ref, gi, extra):
        for h, piece in enumerate(quarters(project(gi))):
            if extra is None:
                ref[h] = piece
            else:
                ref[h, :, 0:hd] = piece
                ref[h, :, hd:] = extra

    put_heads(sbq, 0, None)
    put_heads(sbk, 1, None)
    put_heads(sbv, 2, None)
    put_heads(dfq, 3, None)
    put_heads(dfk, 4, None)
    put_heads(dfv, 5, ones_lane)
    put_heads(mbq, 6, zeros)
    put_heads(mbk, 7, mhot_ref[...])
    put_heads(mbv, 8, ones_lane)

    for h, piece in enumerate(quarters(project(9))):
        for qt in range(tm // t):
            nsq[qt, h * t:(h + 1) * t, 0:hd] = piece[qt * t:(qt + 1) * t]
            nsq[qt, h * t:(h + 1) * t, hd:] = zeros[0:t]

    kc_tok, vc_tok, k_slc, v_slc = quarters(project(10))
    kct[...] = kc_tok
    vct[...] = vc_tok
    ksa[:, 0:hd] = k_slc
    ksa[:, hd:] = shot_ref[...]
    vs1[:, 0:hd] = v_slc
    vs1[:, hd:] = ones_lane

    y = project(11)
    kwp[:, 0:hd] = y[:, 0:hd].astype(BF16)
    kwp[:, hd:] = zeros
    vw1[:, 0:hd] = y[:, hd:2 * hd].astype(BF16)
    vw1[:, hd:] = ones_lane
    glog[...] = y[:, 2 * hd:].astype(BF16)


def _inproj(x2, g, w, gain, nflag, rflag, consts, batch, seq, tm):
    n, d = x2.shape
    ncols = N_GROUPS * GROUP
    nst = seq // tm
    t = ATT_TILE
    row = lambda i: (i, 0)
    fix2 = lambda i: (0, 0)
    fix3 = lambda i: (0, 0, 0)
    tab = lambda i: (i % nst, 0)
    vec = _resident((N_GROUPS, 1, GROUP), fix3)
    tspec = pl.BlockSpec((tm, GROUP), tab)
    hspec = pl.BlockSpec((tm, HEAD_DIM), tab)

    def heads_out(width):
        return (pl.BlockSpec((None, N_HEADS, tm, width), lambda i: (i // nst, 0, i % nst, 0)),
                jax.ShapeDtypeStruct((batch, N_HEADS, seq, width), BF16))

    def tokens_out(width):
        return (pl.BlockSpec((None, tm, width), lambda i: (i // nst, i % nst, 0)),
                jax.ShapeDtypeStruct((batch, seq, width), BF16))

    stacked_q = (pl.BlockSpec((None, tm // t, N_HEADS * t, LANES), lambda i: (i // nst, i % nst, 0, 0)),
                 jax.ShapeDtypeStruct((batch, seq // t, N_HEADS * t, LANES), BF16))
    outs = ([heads_out(HEAD_DIM)] * 5 + [heads_out(LANES)] * 4 + [stacked_q]
            + [tokens_out(HEAD_DIM)] * 2 + [tokens_out(LANES)] * 5)
    return pl.pallas_call(
        _inproj_kernel,
        grid=(n // tm,),
        in_specs=[pl.BlockSpec((tm, d), row), _resident((1, d), fix2), _resident((d, ncols), fix2),
                  vec, vec, vec, _resident((GROUP, GROUP), fix2), _resident((GROUP, GROUP), fix2),
                  tspec, tspec, tspec, tspec, tspec, tspec, hspec, hspec],
        out_specs=[o[0] for o in outs],
        out_shape=[o[1] for o in outs],
        compiler_params=_params("parallel"),
        name="inproj",
    )(x2, g, w, gain, nflag, rflag, _group_mean_matrix(HEAD_DIM, GROUP),
      _group_mean_matrix(DIFF_DIM, GROUP), *consts["tabs64"], *consts["tabs32"],
      consts["moba_hot"], consts["slc_hot"])


def _col_minus_row(nrows, row_mod=None, width=ATT_TILE):
    shape = (nrows, width)
    row = lax.broadcasted_iota(jnp.int32, shape, 0)
    if row_mod is not None:
        row = row & (row_mod - 1)
    return lax.broadcasted_iota(jnp.int32, shape, 1) - row


class _Keys(NamedTuple):
    starts: list
    masks: list
    limit: Any = None
    row_mod: Any = None
    floor: Any = None


def _causal_sweep(step, i, rows, tq, strict):
    t = ATT_TILE
    per_block = tq // t
    base = i * per_block
    tile = lambda j: pl.multiple_of(j * t, t)
    visible = (lambda cmr, lim: cmr < lim) if strict else (lambda cmr, lim: cmr <= lim)
    edge = 1 if strict else 0
    if tq == t:
        step(slice(0, rows), _Keys([tile(base)], [visible(_col_minus_row(rows, tq), 0)], -edge, tq))
    else:
        assert rows == tq
        chunk = min(DIAG_ROWS, rows)
        cmr = _col_minus_row(chunk)
        for r0 in range(0, rows, chunk):
            starts, masks = [], []
            for d in range((r0 + chunk - 1) // t, -1, -1):
                starts.append(tile(base + d))
                some_key_ahead = d * t + t - 1 > r0 - edge
                masks.append(visible(cmr, r0 - d * t) if some_key_ahead else None)
            step(slice(r0, r0 + chunk), _Keys(starts, masks, r0 - edge))

    groups = base // UNROLL

    def body(g, carry):
        newest = base - 1 - g * UNROLL
        step(slice(0, rows), _Keys([tile(newest - u) for u in range(UNROLL)], [None] * UNROLL))
        return carry

    lax.fori_loop(0, groups, body, 0)
    if per_block % UNROLL == 0:
        return
    left = base - groups * UNROLL
    for r in range(1, UNROLL):
        @pl.when(left == r)
        def _():
            step(slice(0, rows), _Keys([tile(r - 1 - u) for u in range(r)], [None] * r))


def _sb_kernel(q_ref, k_ref, v_ref, o_ref, carry_ref, acc_ref, *, tq):
    t = ATT_TILE
    i = pl.program_id(1)
    rows = q_ref.shape[0]
    r = lax.broadcasted_iota(jnp.int32, (t, t), 0)
    c = lax.broadcasted_iota(jnp.int32, (t, t), 1)
    suffix = (r > c).astype(BF16)

    def step(rs, keys):
        q = q_ref[rs, :]
        carry = carry_ref[rs, :]
        pv = None
        for start, mask in zip(keys.starts, keys.masks):
            z = _dot_nt(q, k_ref[pl.ds(start, t), :])
            sp = jnp.maximum(z, 0.0) + jnp.log(1.0 + jnp.exp2(jnp.abs(z) * (-LOG2_E)))
            if mask is not None:
                sp = jnp.where(mask, sp, 0.0)
            later = _dot(sp.astype(BF16), suffix)
            a = jnp.exp2((z - sp - later - carry) * LOG2_E)
            if mask is not None:
                a = jnp.where(mask, a, 0.0)
            d = _dot(a.astype(BF16), v_ref[pl.ds(start, t), :])
            pv = d if pv is None else pv + d
            carry = carry + (later[:, 0:1] + sp[:, 0:1])
        acc_ref[rs, :] += pv
        carry_ref[rs, :] = carry

    carry_ref[...] = jnp.zeros_like(carry_ref)
    acc_ref[...] = jnp.zeros_like(acc_ref)
    _causal_sweep(step, i, rows, tq, True)
    o_ref[...] = acc_ref[...].astype(o_ref.dtype)


def _q_block(seq):
    return min(4 * ATT_TILE, seq)


def _head_block(rows, width):
    if rows is None:
        return lambda seq: pl.BlockSpec((None, None, seq, width),
                                        lambda bh, i: (bh // N_HEADS, bh % N_HEADS, 0, 0))
    return pl.BlockSpec((None, None, rows, width), lambda bh, i: (bh // N_HEADS, bh % N_HEADS, i, 0))


def _stick_breaking(q, k, v, batch, seq):
    tq = _q_block(seq)
    return pl.pallas_call(
        functools.partial(_sb_kernel, tq=tq),
        grid=(batch * N_HEADS, seq // tq),
        in_specs=[_head_block(tq, HEAD_DIM), _head_block(None, HEAD_DIM)(seq), _head_block(None, HEAD_DIM)(seq)],
        out_specs=_head_block(tq, HEAD_DIM),
        out_shape=jax.ShapeDtypeStruct((batch, N_HEADS, seq, HEAD_DIM), BF16),
        scratch_shapes=[pltpu.VMEM((tq, 1), F32), pltpu.VMEM((tq, HEAD_DIM), F32)],
        compiler_params=_params("parallel", "arbitrary"),
        name="stick_breaking",
    )(q, k, v)


def _softmax_step(q, k_ref, v_ref, keys, rs, m_ref, acc_ref):
    t = ATT_TILE
    if m_ref is None:
        width = t * len(keys.starts)
        oldest = keys.starts[-1]
        s = _dot_nt(q, k_ref[pl.ds(oldest, width), :])
        if keys.limit is not None:
            cmr = _col_minus_row(q.shape[0], keys.row_mod, width)
            seen = cmr <= keys.limit
            if keys.floor is not None:
                seen = seen & (cmr > keys.floor)
            s = jnp.where(seen, s, NEG_INF)
        acc_ref[rs, :] += _dot(jnp.exp2(s).astype(BF16), v_ref[pl.ds(oldest, width), :])
        return
    for start, mask in reversed(list(zip(keys.starts, keys.masks))):
        s = _dot_nt(q, k_ref[pl.ds(start, t), :])
        if mask is not None:
            s = jnp.where(mask, s, NEG_INF)
        v1 = v_ref[pl.ds(start, t), :]
        m_old = m_ref[rs, :]
        mn = jnp.maximum(m_old, jnp.max(s, axis=1, keepdims=True))
        acc_ref[rs, :] = jnp.exp2(m_old - mn) * acc_ref[rs, :] + _dot(jnp.exp2(s - mn).astype(BF16), v1)
        m_ref[rs, :] = mn


def _either_path(bound_ref, m_refs, sweep):
    bounded = bound_ref[0] <= EXP2_LIMIT

    @pl.when(bounded)
    def _():
        sweep((None,) * len(m_refs))

    @pl.when(jnp.logical_not(bounded))
    def _():
        for ref in m_refs:
            ref[...] = jnp.full(ref.shape, NEG_INF, F32)
        sweep(m_refs)


def _normalised(acc):
    return acc[:, :HEAD_DIM] / acc[:, HEAD_DIM:HEAD_DIM + 1]


def _diff_kernel(bound_ref, q_ref, k_ref, v_ref, lam_ref, g_ref, o_ref, q1_ref, q2_ref, m1_ref, m2_ref,
                 a1_ref, a2_ref, *, lam_init, tq):
    i = pl.program_id(1)
    rows = q_ref.shape[0]
    q = q_ref[...]
    lane = lax.broadcasted_iota(jnp.int32, q.shape, 1)
    zero = jnp.zeros_like(q)
    q1_ref[...] = jnp.where(lane < DIFF_DIM, q, zero)
    q2_ref[...] = jnp.where(lane >= DIFF_DIM, q, zero)
    for ref in (a1_ref, a2_ref):
        ref[...] = jnp.zeros_like(ref)

    def sweep(m_refs):
        def step(rs, keys):
            _softmax_step(q1_ref[rs, :], k_ref, v_ref, keys, rs, m_refs[0], a1_ref)
            _softmax_step(q2_ref[rs, :], k_ref, v_ref, keys, rs, m_refs[1], a2_ref)

        _causal_sweep(step, i, rows, tq, False)

    _either_path(bound_ref, (m1_ref, m2_ref), sweep)

    lp = lam_ref[...]
    lam = (jnp.exp(jnp.sum(lp[0:1, :] * lp[1:2, :], axis=1, keepdims=True))
           - jnp.exp(jnp.sum(lp[2:3, :] * lp[3:4, :], axis=1, keepdims=True)) + lam_init)
    o = _normalised(a1_ref[...]) - lam * _normalised(a2_ref[...])
    o = o * lax.rsqrt(jnp.mean(o * o, axis=-1, keepdims=True) + RMS_EPS) * g_ref[...]
    o_ref[...] = (o * (1.0 - lam_init)).astype(o_ref.dtype)


_SCALAR = pl.BlockSpec(memory_space=pltpu.SMEM)


def _diff_attention(bound, q, k, v1, lam_p, subln_g, batch, seq, lam_init):
    tq = _q_block(seq)
    fix = lambda bh, i: (0, 0)
    return pl.pallas_call(
        functools.partial(_diff_kernel, lam_init=lam_init, tq=tq),
        grid=(batch * N_HEADS, seq // tq),
        in_specs=[_SCALAR, _head_block(tq, HEAD_DIM), _head_block(None, HEAD_DIM)(seq),
                  _head_block(None, LANES)(seq),
                  pl.BlockSpec((4, DIFF_DIM), fix), pl.BlockSpec((1, HEAD_DIM), fix)],
        out_specs=_head_block(tq, HEAD_DIM),
        out_shape=jax.ShapeDtypeStruct((batch, N_HEADS, seq, HEAD_DIM), BF16),
        scratch_shapes=[pltpu.VMEM((tq, HEAD_DIM), BF16), pltpu.VMEM((tq, HEAD_DIM), BF16),
                        pltpu.VMEM((tq, 1), F32), pltpu.VMEM((tq, 1), F32),
                        pltpu.VMEM((tq, LANES), F32), pltpu.VMEM((tq, LANES), F32)],
        compiler_params=_params("parallel", "arbitrary"),
        name="diff_attention",
    )(bound, q, k, v1, lam_p, subln_g)


def _select_bias(score_t, keep_extra_t, ntop):
    nblk = score_t.shape[0]
    sub = lax.broadcasted_iota(jnp.int32, score_t.shape, 0)
    rank = jnp.zeros(score_t.shape, jnp.int32)
    for b in range(nblk):
        row = score_t[b:b + 1, :]
        ahead = (row > score_t) | ((row == score_t) & (sub > b))
        rank = rank + ahead.astype(jnp.int32)
    keep = rank < ntop
    if keep_extra_t is not None:
        keep = keep_extra_t(keep, sub)
    return jnp.where(keep, 0.0, NEG_INF)


def _bias_lanes(bias_t, lane0):
    nblk, t = bias_t.shape
    parts = []
    if lane0:
        parts.append(jnp.zeros((lane0, t), F32))
    parts.append(bias_t)
    if LANES - lane0 - nblk:
        parts.append(jnp.zeros((LANES - lane0 - nblk, t), F32))
    return jnp.concatenate(parts, axis=0).T


def _moba_gate_kernel(q_ref, k_ref, a_ref, o_ref, km_hi, km_lo, *, nblk):
    i = pl.program_id(1)

    @pl.when(i == 0)
    def _():
        kmean = _dot(a_ref[...], k_ref[...])
        hi, lo = _split_bf16(kmean)
        km_hi[...] = hi
        km_lo[...] = lo

    q = q_ref[...]
    rows = q.shape[0]
    gate = _dot_nt(q, km_hi[...]) + _dot_nt(q, km_lo[...])
    gate_t = gate.T[HEAD_DIM:HEAD_DIM + nblk, :]
    blk = lax.broadcasted_iota(jnp.int32, gate_t.shape, 0)
    own = (i * rows + lax.broadcasted_iota(jnp.int32, gate_t.shape, 1)) // MOBA_BLOCK
    score_t = jnp.where(blk < own, gate_t, NEG_INF)
    ksel = min(MOBA_TOPK, nblk - 1)

    def keep_rule(keep, sub):
        return (keep & (sub < own)) | (sub == own)

    bias_t = _select_bias(score_t, keep_rule, ksel)
    o_ref[...] = (q.astype(F32) + _bias_lanes(bias_t, HEAD_DIM)).astype(o_ref.dtype)


def _moba_gate(q_pad, k_aug, amat, batch, seq):
    tq = _q_block(seq)
    nblk = seq // MOBA_BLOCK
    return pl.pallas_call(
        functools.partial(_moba_gate_kernel, nblk=nblk),
        grid=(batch * N_HEADS, seq // tq),
        in_specs=[_head_block(tq, LANES), _head_block(None, LANES)(seq),
                  pl.BlockSpec((LANES, seq), lambda bh, i: (0, 0))],
        out_specs=_head_block(tq, LANES),
        out_shape=jax.ShapeDtypeStruct((batch, N_HEADS, seq, LANES), BF16),
        scratch_shapes=[pltpu.VMEM((LANES, LANES), BF16), pltpu.VMEM((LANES, LANES), BF16)],
        compiler_params=_params("parallel", "arbitrary"),
        name="moba_gate",
    )(q_pad, k_aug, amat)


def _flash_kernel(bound_ref, q_ref, k_ref, v_ref, o_ref, m_ref, acc_ref, *, tq, window_tiles):
    t = ATT_TILE
    i = pl.program_id(1)
    rows = q_ref.shape[0]
    acc_ref[...] = jnp.zeros_like(acc_ref)

    def sweep(m_refs):
        def step(rs, keys):
            _softmax_step(q_ref[rs, :], k_ref, v_ref, keys, rs, m_refs[0], acc_ref)

        if window_tiles is None:
            _causal_sweep(step, i, rows, tq, False)
            return
        assert tq == t
        everything = slice(0, rows)
        tile = lambda back: pl.multiple_of((i - back) * t, t)
        cmr = _col_minus_row(rows, tq)

        @pl.when(i >= window_tiles)
        def _():
            top = window_tiles * t
            step(everything, _Keys([tile(b) for b in range(window_tiles + 1)],
                                   [cmr <= 0] + [None] * (window_tiles - 1) + [cmr > top - NSA_WINDOW],
                                   top, tq, floor=top - NSA_WINDOW))

        @pl.when(i < window_tiles)
        def _():
            step(everything, _Keys([tile(0)], [cmr <= 0], 0, tq))
            for back in range(1, window_tiles):
                @pl.when(i >= back)
                def _():
                    step(everything, _Keys([tile(back)], [None]))

    _either_path(bound_ref, (m_ref,), sweep)
    o_ref[...] = _normalised(acc_ref[...]).astype(o_ref.dtype)


def _flash(bound, q, k, v1, *, q_map, kv_map, o_map, grid, rows, tq, seq, out_lead, out_dtype, window_tiles,
           name):
    lead = (None,) * (q.ndim - 2)
    klead = (None,) * (k.ndim - 2)
    return pl.pallas_call(
        functools.partial(_flash_kernel, tq=tq, window_tiles=window_tiles),
        grid=grid,
        in_specs=[_SCALAR, pl.BlockSpec(lead + (rows, LANES), q_map),
                  pl.BlockSpec(klead + (seq, LANES), kv_map),
                  pl.BlockSpec(klead + (seq, LANES), kv_map)],
        out_specs=pl.BlockSpec(lead + (rows, HEAD_DIM), o_map),
        out_shape=jax.ShapeDtypeStruct(out_lead + (HEAD_DIM,), out_dtype),
        scratch_shapes=[pltpu.VMEM((rows, 1), F32), pltpu.VMEM((rows, LANES), F32)],
        compiler_params=_params("parallel", "arbitrary"),
        name=name,
    )(bound, q, k, v1)


def _nsa_compress_kernel(tk_ref, tv_ref, pe_ref, wk_ref, wv_ref, g_ref, cos_ref, sa_ref, sb_ref,
                         kc_ref, vc_ref):
    rows = tk_ref.shape[0]

    def compress(tok, pe_lo, pe_hi, w_ref):
        half = w_ref.shape[0] // 2
        a = _dot((tok + pe_lo).astype(BF16), w_ref[0:half, :])
        b = _dot((tok + pe_hi).astype(BF16), w_ref[half:, :])
        return a + pltpu.roll(b, rows - 1, 0)

    kc = compress(tk_ref[...].astype(F32), pe_ref[0:1, :], pe_ref[1:2, :], wk_ref)
    vc = compress(tv_ref[...].astype(F32), pe_ref[2:3, :], pe_ref[3:4, :], wv_ref)
    ms = jnp.sum(kc * kc, axis=-1, keepdims=True) / HEAD_DIM
    kc = kc * lax.rsqrt(ms + RMS_EPS) * g_ref[...]
    kc = _rope_apply(kc, cos_ref[...], sa_ref[...], sb_ref[...], HEAD_DIM // ROPE_FRACTION_DEN // 2)
    kc_ref[...] = kc.astype(kc_ref.dtype)
    vc_ref[...] = vc.astype(vc_ref.dtype)


def _nsa_compress(tk, tv, pe4, wk, wv, gk, tabs, batch):
    rows, width = tk.shape[1], tk.shape[2]
    b3 = lambda b: (b, 0, 0)
    fix = lambda b: (0, 0)
    tok = pl.BlockSpec((None, rows, width), b3)
    out = pl.BlockSpec((None, rows, LANES), b3)
    tab = pl.BlockSpec((rows, LANES), fix)
    return pl.pallas_call(
        _nsa_compress_kernel,
        grid=(batch,),
        in_specs=[tok, tok, pl.BlockSpec((4, width), fix),
                  pl.BlockSpec((2 * width, LANES), fix), pl.BlockSpec((2 * width, LANES), fix),
                  pl.BlockSpec((1, LANES), fix), tab, tab, tab],
        out_specs=[out, out],
        out_shape=[jax.ShapeDtypeStruct((batch, rows, LANES), BF16)] * 2,
        compiler_params=_params("parallel"),
        name="nsa_compress",
    )(tk, tv, pe4, wk, wv, gk, *tabs)


def _nsa_select_kernel(q_ref, kc_ref, vc_ref, m_ref, qa_ref, oc_ref, *, nslc, ntop):
    t = ATT_TILE
    i = pl.program_id(1)
    q = q_ref[...]
    rows = q.shape[0]
    ncmp = kc_ref.shape[0]
    s = _dot_nt(q, kc_ref[...])
    qpos = i * t + (lax.broadcasted_iota(jnp.int32, (rows, ncmp), 0) & (t - 1))
    cmp_end = NSA_CMP_STRIDE * lax.broadcasted_iota(jnp.int32, (rows, ncmp), 1) + (NSA_CMP_LEN - 1)
    mask = cmp_end <= qpos
    s = jnp.where(mask, s, NEG_INF)
    e = jnp.where(mask, jnp.exp2(s - jnp.max(s, axis=1, keepdims=True)), 0.0)
    p = e / jnp.maximum(jnp.sum(e, axis=1, keepdims=True), 1e-30)
    oc_ref[...] = _dot(p.astype(BF16), vc_ref[...])[:, :HEAD_DIM].astype(oc_ref.dtype)

    psum = p[0:t]
    for h in range(1, rows // t):
        psum = psum + p[h * t:(h + 1) * t]
    hi, lo = _split_bf16(psum)
    imp = _dot(hi, m_ref[...]) + _dot(lo, m_ref[...])
    imp_t = imp.T[HEAD_DIM:HEAD_DIM + nslc, :]
    blk = lax.broadcasted_iota(jnp.int32, imp_t.shape, 0)
    cur = (i * t + lax.broadcasted_iota(jnp.int32, imp_t.shape, 1)) // NSA_SLC_BLOCK
    forced = (blk == 0) | (blk == cur) | (blk == cur - 1)
    allowed = blk <= cur
    score_t = jnp.where(allowed, jnp.where(forced, BIG, imp_t), NEG_INF)
    bias = _bias_lanes(_select_bias(score_t, None, ntop), HEAD_DIM)
    qf = q.astype(F32)
    for h in range(rows // t):
        qa_ref[h * t:(h + 1) * t, :] = (qf[h * t:(h + 1) * t] + bias).astype(qa_ref.dtype)


def _nsa_select(q4, kc, vc, mmat, batch, seq):
    t = ATT_TILE
    rows = N_HEADS * t
    ncmp = kc.shape[1]
    nslc = seq // NSA_SLC_BLOCK
    qmap = lambda b, i: (b, i, 0, 0)
    kmap = lambda b, i: (b, 0, 0)
    return pl.pallas_call(
        functools.partial(_nsa_select_kernel, nslc=nslc, ntop=min(NSA_TOPN, nslc)),
        grid=(batch, seq // t),
        in_specs=[pl.BlockSpec((None, None, rows, LANES), qmap),
                  pl.BlockSpec((None, ncmp, LANES), kmap), pl.BlockSpec((None, ncmp, LANES), kmap),
                  pl.BlockSpec((ncmp, LANES), lambda b, i: (0, 0))],
        out_specs=[pl.BlockSpec((None, None, rows, LANES), qmap),
                   pl.BlockSpec((None, None, rows, HEAD_DIM), qmap)],
        out_shape=[jax.ShapeDtypeStruct((batch, seq // t, rows, LANES), BF16),
                   jax.ShapeDtypeStruct((batch, seq // t, rows, HEAD_DIM), F32)],
        compiler_params=_params("parallel", "arbitrary"),
        name="nsa_select",
    )(q4, kc, vc, mmat)


def _merge_kernel(x_ref, g_ref, oa_ref, ob_ref, oc_ref, dc_ref, ds_ref, dw_ref, gl_ref, e_ref,
                  wg_ref, bg_ref, wb_ref, wo_ref, y_ref):
    t = ATT_TILE
    x = x_ref[...]
    xn = x * lax.rsqrt(jnp.mean(x * x, axis=-1, keepdims=True) + RMS_EPS) * g_ref[...]
    xn = xn.astype(BF16)

    def token_major(ref):
        return jnp.concatenate([ref[h] for h in range(N_HEADS)], axis=1)

    def unstack(ref):
        return jnp.concatenate(
            [jnp.concatenate([ref[qt, h * t:(h + 1) * t, :] for h in range(N_HEADS)], axis=1)
             for qt in range(ref.shape[0])], axis=0)

    sig = 1.0 / (1.0 + jnp.exp(-gl_ref[...].astype(F32)))
    hi, lo = _split_bf16(sig)
    gexp = _dot(hi, e_ref[...]) + _dot(lo, e_ref[...])
    w = MIX_WIDTH
    o_d = (gexp[:, 0:w] * unstack(dc_ref) + gexp[:, w:2 * w] * unstack(ds_ref)
           + gexp[:, 2 * w:3 * w] * unstack(dw_ref)).astype(BF16)
    merged = None
    for bi, o in enumerate((token_major(oa_ref), token_major(ob_ref), token_major(oc_ref), o_d)):
        gate = 1.0 / (1.0 + jnp.exp(-(_dot(xn, wg_ref[bi]) + bg_ref[bi])))
        term = gate * _dot(o, wb_ref[bi])
        merged = term if merged is None else merged + term
    y_ref[...] = x + _dot(merged.astype(BF16), wo_ref[...])


def _merge(x2, g, oa, ob, oc, dc, ds, dw, glog, emat, wg, bg, wb, wo, seq, tm):
    n, d = x2.shape
    nst = seq // tm
    t = ATT_TILE
    row = lambda i: (i, 0)
    fix2 = lambda i: (0, 0)
    fix3 = lambda i: (0, 0, 0)
    heads = pl.BlockSpec((None, N_HEADS, tm, HEAD_DIM), lambda i: (i // nst, 0, i % nst, 0))
    stacked = pl.BlockSpec((None, tm // t, N_HEADS * t, HEAD_DIM), lambda i: (i // nst, i % nst, 0, 0))
    return pl.pallas_call(
        _merge_kernel,
        grid=(n // tm,),
        in_specs=[pl.BlockSpec((tm, d), row), _resident((1, d), fix2), heads, heads, heads,
                  stacked, stacked, stacked,
                  pl.BlockSpec((None, tm, LANES), lambda i: (i // nst, i % nst, 0)),
                  _resident((LANES, 3 * MIX_WIDTH), fix2),
                  _resident((4, d, d), fix3), _resident((4, 1, d), fix3),
                  _resident((4, MIX_WIDTH, d), fix3), _resident((d, d), fix2)],
        out_specs=pl.BlockSpec((tm, d), row),
        out_shape=jax.ShapeDtypeStruct((n, d), F32),
        compiler_params=_params("parallel"),
        name="gated_merge",
    )(x2, g, oa, ob, oc, dc, ds, dw, glog, emat, wg, bg, wb, wo)


def _ffn_kernel(x_ref, g_ref, wg_ref, wu_ref, wd_ref, y_ref):
    x = x_ref[...]
    hn = x * lax.rsqrt(jnp.mean(x * x, axis=-1, keepdims=True) + RMS_EPS) * g_ref[...]
    hn = hn.astype(BF16)
    gate = _dot(hn, wg_ref[...])
    up = _dot(hn, wu_ref[...])
    act = gate * (1.0 / (1.0 + jnp.exp(-gate))) * up
    y_ref[...] = x + _dot(act.astype(BF16), wd_ref[...])


def _ffn(x2, g, wg, wu, wd, tm):
    n, d = x2.shape
    f = wg.shape[1]
    row = lambda i: (i, 0)
    fix = lambda i: (0, 0)
    return pl.pallas_call(
        _ffn_kernel,
        grid=(n // tm,),
        in_specs=[pl.BlockSpec((tm, d), row), _resident((1, d), fix), _resident((d, f), fix),
                  _resident((d, f), fix), _resident((f, d), fix)],
        out_specs=pl.BlockSpec((tm, d), row),
        out_shape=jax.ShapeDtypeStruct((n, d), F32),
        compiler_params=_params("parallel"),
        name="swiglu",
    )(x2, g, wg, wu, wd)


def _layer_constants(seq):
    t = ATT_TILE
    pos = jnp.arange(seq, dtype=jnp.int32)
    tabs64 = _rope_tables(pos, HEAD_DIM, GROUP)
    tabs32 = _rope_tables(pos, DIFF_DIM, GROUP)
    ncmp = seq // NSA_CMP_STRIDE
    cmp_end = NSA_CMP_STRIDE * jnp.arange(ncmp, dtype=jnp.int32) + (NSA_CMP_LEN - 1)
    tabs_cmp = _rope_tables(cmp_end, HEAD_DIM, HEAD_DIM)
    tabs_cmp = tuple(jnp.pad(a, ((0, 0), (0, LANES - HEAD_DIM)), constant_values=c)
                     for a, c in zip(tabs_cmp, (1.0, 0.0, 0.0)))

    s = np.arange(seq)
    nblk = seq // MOBA_BLOCK
    nslc = seq // NSA_SLC_BLOCK
    assert nblk <= LANES - HEAD_DIM and nslc <= LANES - HEAD_DIM and seq % t == 0
    moba_hot = np.zeros((seq, LANES - HEAD_DIM), np.float32)
    moba_hot[s, s // MOBA_BLOCK] = 1.0
    slc_hot = np.zeros((seq, LANES - HEAD_DIM), np.float32)
    slc_hot[s, s // NSA_SLC_BLOCK] = 1.0
    amat = np.zeros((LANES, seq), np.float32)
    amat[HEAD_DIM + s // MOBA_BLOCK, s] = 1.0 / MOBA_BLOCK

    nc_real = (seq - NSA_CMP_LEN) // NSA_CMP_STRIDE + 1
    cstart = NSA_CMP_STRIDE * np.arange(ncmp)
    sstart = NSA_SLC_BLOCK * np.arange(nslc)
    overlap = np.clip(np.minimum(cstart[:, None] + NSA_CMP_LEN, sstart[None, :] + NSA_SLC_BLOCK)
                      - np.maximum(cstart[:, None], sstart[None, :]), 0, None)
    mmat = np.zeros((ncmp, LANES), np.float32)
    mmat[:nc_real, HEAD_DIM:HEAD_DIM + nslc] = overlap[:nc_real].astype(np.float32) / NSA_CMP_STRIDE

    emat = np.zeros((LANES, 3 * MIX_WIDTH), np.float32)
    for br in range(3):
        for hd in range(N_HEADS):
            emat[br * N_HEADS + hd,
                 br * MIX_WIDTH + hd * HEAD_DIM: br * MIX_WIDTH + (hd + 1) * HEAD_DIM] = 1.0

    return dict(tabs64=tabs64, tabs32=tabs32, tabs_cmp=tabs_cmp,
                moba_hot=jnp.asarray(moba_hot, BF16), slc_hot=jnp.asarray(slc_hot, BF16),
                amat=jnp.asarray(amat, BF16), mmat=jnp.asarray(mmat, BF16), emat=jnp.asarray(emat, BF16))


def _lane_vectors(diff_qn_g, diff_kn_g, moba_qn_g, moba_kn_g, nsa_qn_g, nsa_kn_g):
    ones = jnp.ones((GROUP,), F32)
    zeros = jnp.zeros((GROUP,), F32)
    sc64_log2 = LOG2_E / math.sqrt(HEAD_DIM)
    sc32_log2 = LOG2_E / math.sqrt(DIFF_DIM)
    quarter = lambda g, k: jnp.concatenate([g if j == k else jnp.ones((HEAD_DIM,), F32) for j in range(4)])
    flag = lambda k: jnp.concatenate([jnp.full((HEAD_DIM,), 1.0 if j == k else 0.0, F32) for j in range(4)])
    gains = [ones / math.sqrt(HEAD_DIM), ones, ones,
             jnp.tile(diff_qn_g, GROUP // DIFF_DIM) * sc32_log2, jnp.tile(diff_kn_g, GROUP // DIFF_DIM), ones,
             jnp.tile(moba_qn_g, N_HEADS) * sc64_log2, jnp.tile(moba_kn_g, N_HEADS), ones,
             jnp.tile(nsa_qn_g, N_HEADS) * sc64_log2,
             quarter(nsa_kn_g[1], 2), quarter(nsa_kn_g[2], 0)]
    flags = [zeros, zeros, zeros, ones, ones, zeros, ones, ones, zeros, ones, flag(2), flag(0)]
    shape = (N_GROUPS, 1, GROUP)
    return jnp.stack(gains).reshape(shape), jnp.stack(flags).reshape(shape), jnp.stack(flags).reshape(shape)


def _row_tile(n, seq, want):
    tm = want if (n % want == 0 and seq % want == 0) else ATT_TILE
    return min(tm, seq)


def _mixers(x2, lw, consts, batch, seq, lam_init):
    n, d = x2.shape
    t = ATT_TILE
    nq = seq // t
    tm = _row_tile(n, seq, 512)

    gain, nflag, rflag = _lane_vectors(lw["diff_qn_g"], lw["diff_kn_g"], lw["moba_qn_g"],
                                       lw["moba_kn_g"], lw["nsa_qn_g"], lw["nsa_kn_g"])
    w_in = jnp.pad(lw["w_in"], ((0, 0), (0, N_GROUPS * GROUP - lw["w_in"].shape[1]))).astype(BF16)
    (sbq, sbk, sbv, dfq, dfk, dfv1, mbq, mbk, mbv1, nsq, kct, vct, ksa, vs1, kwp, vw1, glog) = _inproj(
        x2, lw["attn_norm_g"].reshape(1, d), w_in, gain, nflag, rflag, consts, batch, seq, tm)

    amax = lambda g: jnp.max(jnp.abs(g))
    score_bound = lambda d_head, gq, gk: (math.sqrt(d_head) * LOG2_E * NORM_SLACK * NORM_SLACK
                                          * amax(gq) * amax(gk)).reshape(1).astype(F32)

    o_a = _stick_breaking(sbq, sbk, sbv, batch, seq)
    o_b = _diff_attention(score_bound(DIFF_DIM, lw["diff_qn_g"], lw["diff_kn_g"]), dfq, dfk, dfv1,
                          lw["diff_lam"], lw["diff_subln_g"].reshape(1, HEAD_DIM), batch, seq, lam_init)

    mq_aug = _moba_gate(mbq, mbk, consts["amat"], batch, seq)
    bh_map = lambda bh, i: (bh // N_HEADS, bh % N_HEADS, i, 0)
    bh_full = lambda bh, i: (bh // N_HEADS, bh % N_HEADS, 0, 0)
    tq = _q_block(seq)
    o_c = _flash(score_bound(HEAD_DIM, lw["moba_qn_g"], lw["moba_kn_g"]), mq_aug, mbk, mbv1,
                 q_map=bh_map, kv_map=bh_full, o_map=bh_map,
                 grid=(batch * N_HEADS, seq // tq), rows=tq, tq=tq, seq=seq,
                 out_lead=(batch, N_HEADS, seq), out_dtype=BF16, window_tiles=None, name="moba_attention")

    ncmp = seq // NSA_CMP_STRIDE
    tok = lambda a: a.reshape(batch, ncmp, NSA_CMP_STRIDE * HEAD_DIM)
    pad64 = lambda a: jnp.pad(a, [(0, 0)] * (a.ndim - 1) + [(0, LANES - HEAD_DIM)])
    pe4 = lw["nsa_cmp_pe"].reshape(4, NSA_CMP_STRIDE * HEAD_DIM)
    wc = pad64(lw["nsa_cmp_w"]).astype(BF16)
    gk = pad64(lw["nsa_kn_g"][0].reshape(1, HEAD_DIM))
    kc, vc = _nsa_compress(tok(kct), tok(vct), pe4, wc[0], wc[1], gk, consts["tabs_cmp"], batch)
    q_aug, d_c = _nsa_select(nsq, kc, vc, consts["mmat"], batch, seq)

    rows = N_HEADS * t
    b_map = lambda b, i: (b, i, 0, 0)
    b_full = lambda b, i: (b, 0, 0)
    nsa_flash = functools.partial(_flash, q_map=b_map, kv_map=b_full, o_map=b_map, grid=(batch, nq),
                                  rows=rows, tq=t, seq=seq, out_lead=(batch, nq, rows), out_dtype=F32)
    d_s = nsa_flash(score_bound(HEAD_DIM, lw["nsa_qn_g"], lw["nsa_kn_g"][1]), q_aug, ksa, vs1,
                    window_tiles=None, name="nsa_selected")
    d_w = nsa_flash(score_bound(HEAD_DIM, lw["nsa_qn_g"], lw["nsa_kn_g"][2]), nsq, kwp, vw1,
                    window_tiles=NSA_WINDOW // t, name="nsa_window")
    return glog, o_a, o_b, o_c, d_c, d_s, d_w


def _layer(x2, lw, consts, batch, seq, lam_init):
    n, d = x2.shape
    glog, o_a, o_b, o_c, d_c, d_s, d_w = _mixers(x2, lw, consts, batch, seq, lam_init)
    x2 = _merge(x2, lw["attn_norm_g"].reshape(1, d), o_a, o_b, o_c, d_c, d_s, d_w,
                glog, consts["emat"], lw["w_gate"].astype(BF16), lw["b_gate"].reshape(4, 1, d),
                lw["w_branch"].astype(BF16), lw["w_out"].astype(BF16), seq, _row_tile(n, seq, 512))
    return _ffn(x2, lw["ffn_norm_g"].reshape(1, d), lw["w_ffn_gate"].astype(BF16),
                lw["w_ffn_up"].astype(BF16), lw["w_ffn_down"].astype(BF16), _row_tile(n, seq, 256))


def kernel(x, attn_norm_g, w_in, diff_qn_g, diff_kn_g, diff_lam, diff_subln_g, moba_qn_g, moba_kn_g,
           nsa_qn_g, nsa_kn_g, nsa_cmp_pe, nsa_cmp_w, w_gate, b_gate, w_branch, w_out, ffn_norm_g,
           w_ffn_gate, w_ffn_up, w_ffn_down):
    batch, seq, d = x.shape
    weights = dict(attn_norm_g=attn_norm_g, w_in=w_in, diff_qn_g=diff_qn_g, diff_kn_g=diff_kn_g,
                   diff_lam=diff_lam, diff_subln_g=diff_subln_g, moba_qn_g=moba_qn_g, moba_kn_g=moba_kn_g,
                   nsa_qn_g=nsa_qn_g, nsa_kn_g=nsa_kn_g, nsa_cmp_pe=nsa_cmp_pe, nsa_cmp_w=nsa_cmp_w,
                   w_gate=w_gate, b_gate=b_gate, w_branch=w_branch, w_out=w_out, ffn_norm_g=ffn_norm_g,
                   w_ffn_gate=w_ffn_gate, w_ffn_up=w_ffn_up, w_ffn_down=w_ffn_down)
    consts = _layer_constants(seq)
    x2 = x.reshape(batch * seq, d)
    for layer in range(w_in.shape[0]):
        lw = {k: v[layer] for k, v in weights.items()}
        lam_init = 0.8 - 0.6 * math.exp(-0.3 * layer)
        x2 = _layer(x2, lw, consts, batch, seq, lam_init)
    return x2.reshape(batch, seq, d)
```

```python
import functools
import math
from typing import Any, NamedTuple

import numpy as np
import jax
import jax.numpy as jnp
from jax import lax
from jax.experimental import pallas as pl
from jax.experimental.pallas import tpu as pltpu

HEAD_DIM = 64
N_HEADS = 4
MIX_WIDTH = N_HEADS * HEAD_DIM
ROPE_THETA = 500000.0
ROPE_FRACTION_DEN = 4
DIFF_DIM = HEAD_DIM // 2
MOBA_BLOCK = 256
MOBA_TOPK = 3
NSA_CMP_LEN = 32
NSA_CMP_STRIDE = 16
NSA_SLC_BLOCK = 64
NSA_TOPN = 16
NSA_WINDOW = 512
NEG_INF = -1e30
BIG = 1e30
RMS_EPS = 1e-6

LANES = 128
ATT_TILE = 256
UNROLL = 4
DIAG_ROWS = 512
GROUP = 256
N_GROUPS = 12
VMEM_LIMIT = 56 * 1024 * 1024
LOG2_E = math.log2(math.e)
EXP2_LIMIT = 80.0
NORM_SLACK = 1.05

F32 = jnp.float32
BF16 = jnp.bfloat16

_NT = (((1,), (1,)), ((), ()))


def _dot(a, b):
    return jnp.dot(a, b, preferred_element_type=F32)


def _dot_nt(a, b):
    return lax.dot_general(a, b, _NT, preferred_element_type=F32)


def _split_bf16(x):
    hi = x.astype(BF16)
    lo = (x - hi.astype(F32)).astype(BF16)
    return hi, lo


def _params(*sem):
    return pltpu.CompilerParams(dimension_semantics=sem, vmem_limit_bytes=VMEM_LIMIT)


def _resident(shape, index_map):
    return pl.BlockSpec(shape, index_map, pipeline_mode=pl.Buffered(1))


def _rope_tables(pos, d, width):
    r = d // ROPE_FRACTION_DEN
    half = r // 2
    inv = ROPE_THETA ** (-jnp.arange(half, dtype=F32) * 2.0 / r)
    ang = pos.astype(F32)[:, None] * inv[None, :]
    cos, sin = jnp.cos(ang), jnp.sin(ang)
    n = pos.shape[0]
    one = jnp.ones((n, d - r), F32)
    zero_h = jnp.zeros((n, half), F32)
    zero_t = jnp.zeros((n, d - r), F32)
    cos_g = jnp.concatenate([cos, cos, one], axis=1)
    sa_g = jnp.concatenate([-sin, zero_h, zero_t], axis=1)
    sb_g = jnp.concatenate([zero_h, sin, zero_t], axis=1)
    reps = width // d
    return (jnp.tile(cos_g, (1, reps)), jnp.tile(sa_g, (1, reps)), jnp.tile(sb_g, (1, reps)))


def _group_mean_matrix(d, width):
    g = np.arange(width) // d
    return jnp.asarray((g[:, None] == g[None, :]).astype(np.float32) / d, BF16)


def _rope_apply(y, cos, sa, sb, half):
    w = y.shape[-1]
    return y * cos + pltpu.roll(y, w - half, 1) * sa + pltpu.roll(y, half, 1) * sb


_GROUP_CFG = (
    (None, None), (None, None), (None, None),
    (DIFF_DIM, DIFF_DIM), (DIFF_DIM, DIFF_DIM), (None, None),
    (HEAD_DIM, HEAD_DIM), (HEAD_DIM, HEAD_DIM), (None, None),
    (HEAD_DIM, HEAD_DIM),
    (HEAD_DIM, HEAD_DIM),
    (HEAD_DIM, HEAD_DIM),
)


def _inproj_kernel(x_ref, g_ref, w_ref, gain_ref, nflag_ref, rflag_ref, b64_ref, b32_ref,
                   c64_ref, sa64_ref, sb64_ref, c32_ref, sa32_ref, sb32_ref, mhot_ref, shot_ref,
                   sbq, sbk, sbv, dfq, dfk, dfv, mbq, mbk, mbv, nsq, kct, vct, ksa, vs1, kwp, vw1, glog):
    x = x_ref[...]
    tm = x.shape[0]
    t = ATT_TILE
    hd = HEAD_DIM
    xn = x * lax.rsqrt(jnp.mean(x * x, axis=-1, keepdims=True) + RMS_EPS) * g_ref[...]
    xn = xn.astype(BF16)
    zeros = jnp.zeros((tm, hd), BF16)
    ones_lane = jnp.where(lax.broadcasted_iota(jnp.int32, (tm, hd), 1) == 0, 1.0, 0.0).astype(BF16)

    def project(gi):
        norm, rope = _GROUP_CFG[gi]
        y = _dot(xn, w_ref[:, gi * GROUP:(gi + 1) * GROUP])
        gain = gain_ref[gi]
        if norm is not None:
            bmat = b64_ref[...] if norm == HEAD_DIM else b32_ref[...]
            ms = _dot((y * y).astype(BF16), bmat)
            y = y * jnp.where(nflag_ref[gi] > 0, lax.rsqrt(ms + RMS_EPS) * gain, gain)
        else:
            y = y * gain
        if rope is not None:
            if rope == HEAD_DIM:
                cos, sa, sb = c64_ref[...], sa64_ref[...], sb64_ref[...]
            else:
                cos, sa, sb = c32_ref[...], sa32_ref[...], sb32_ref[...]
            rf = rflag_ref[gi]
            cos = jnp.where(rf > 0, cos, 1.0)
            y = _rope_apply(y, cos, sa * rf, sb * rf, rope // ROPE_FRACTION_DEN // 2)
        return y

    def quarters(y):
        return [y[:, h * hd:(h + 1) * hd].astype(BF16) for h in range(N_HEADS)]

    def put_heads(ref, gi, extra):
        for h, piece in enumerate(quarters(project(gi))):
            if extra is None:
                ref[h] = piece
            else:
                ref[h, :, 0:hd] = piece
                ref[h, :, hd:] = extra

    put_heads(sbq, 0, None)
    put_heads(sbk, 1, None)
    put_heads(sbv, 2, None)
    put_heads(dfq, 3, None)
    put_heads(dfk, 4, None)
    put_heads(dfv, 5, ones_lane)
    put_heads(mbq, 6, zeros)
    put_heads(mbk, 7, mhot_ref[...])
    put_heads(mbv, 8, ones_lane)

    for h, piece in enumerate(quarters(project(9))):
        for qt in range(tm // t):
            nsq[qt, h * t:(h + 1) * t, 0:hd] = piece[qt * t:(qt + 1) * t]
            nsq[qt, h * t:(h + 1) * t, hd:] = zeros[0:t]

    kc_tok, vc_tok, k_slc, v_slc = quarters(project(10))
    kct[...] = kc_tok
    vct[...] = vc_tok
    ksa[:, 0:hd] = k_slc
    ksa[:, hd:] = shot_ref[...]
    vs1[:, 0:hd] = v_slc
    vs1[:, hd:] = ones_lane

    y = project(11)
    kwp[:, 0:hd] = y[:, 0:hd].astype(BF16)
    kwp[:, hd:] = zeros
    vw1[:, 0:hd] = y[:, hd:2 * hd].astype(BF16)
    vw1[:, hd:] = ones_lane
    glog[...] = y[:, 2 * hd:].astype(BF16)


def _inproj(x2, g, w, gain, nflag, rflag, consts, batch, seq, tm):
    n, d = x2.shape
    ncols = N_GROUPS * GROUP
    nst = seq // tm
    t = ATT_TILE
    row = lambda i: (i, 0)
    fix2 = lambda i: (0, 0)
    fix3 = lambda i: (0, 0, 0)
    tab = lambda i: (i % nst, 0)
    vec = _resident((N_GROUPS, 1, GROUP), fix3)
    tspec = pl.BlockSpec((tm, GROUP), tab)
    hspec = pl.BlockSpec((tm, HEAD_DIM), tab)

    def heads_out(width):
        return (pl.BlockSpec((None, N_HEADS, tm, width), lambda i: (i // nst, 0, i % nst, 0)),
                jax.ShapeDtypeStruct((batch, N_HEADS, seq, width), BF16))

    def tokens_out(width):
        return (pl.BlockSpec((None, tm, width), lambda i: (i // nst, i % nst, 0)),
                jax.ShapeDtypeStruct((batch, seq, width), BF16))

    stacked_q = (pl.BlockSpec((None, tm // t, N_HEADS * t, LANES), lambda i: (i // nst, i % nst, 0, 0)),
                 jax.ShapeDtypeStruct((batch, seq // t, N_HEADS * t, LANES), BF16))
    outs = ([heads_out(HEAD_DIM)] * 5 + [heads_out(LANES)] * 4 + [stacked_q]
            + [tokens_out(HEAD_DIM)] * 2 + [tokens_out(LANES)] * 5)
    return pl.pallas_call(
        _inproj_kernel,
        grid=(n // tm,),
        in_specs=[pl.BlockSpec((tm, d), row), _resident((1, d), fix2), _resident((d, ncols), fix2),
                  vec, vec, vec, _resident((GROUP, GROUP), fix2), _resident((GROUP, GROUP), fix2),
                  tspec, tspec, tspec, tspec, tspec, tspec, hspec, hspec],
        out_specs=[o[0] for o in outs],
        out_shape=[o[1] for o in outs],
        compiler_params=_params("parallel"),
        name="inproj",
    )(x2, g, w, gain, nflag, rflag, _group_mean_matrix(HEAD_DIM, GROUP),
      _group_mean_matrix(DIFF_DIM, GROUP), *consts["tabs64"], *consts["tabs32"],
      consts["moba_hot"], consts["slc_hot"])


def _col_minus_row(nrows, row_mod=None, width=ATT_TILE):
    shape = (nrows, width)
    row = lax.broadcasted_iota(jnp.int32, shape, 0)
    if row_mod is not None:
        row = row & (row_mod - 1)
    return lax.broadcasted_iota(jnp.int32, shape, 1) - row


class _Keys(NamedTuple):
    starts: list
    masks: list
    limit: Any = None
    row_mod: Any = None
    floor: Any = None


def _causal_sweep(step, i, rows, tq, strict):
    t = ATT_TILE
    per_block = tq // t
    base = i * per_block
    tile = lambda j: pl.multiple_of(j * t, t)
    visible = (lambda cmr, lim: cmr < lim) if strict else (lambda cmr, lim: cmr <= lim)
    edge = 1 if strict else 0
    if tq == t:
        step(slice(0, rows), _Keys([tile(base)], [visible(_col_minus_row(rows, tq), 0)], -edge, tq))
    else:
        assert rows == tq
        chunk = min(DIAG_ROWS, rows)
        cmr = _col_minus_row(chunk)
        for r0 in range(0, rows, chunk):
            starts, masks = [], []
            for d in range((r0 + chunk - 1) // t, -1, -1):
                starts.append(tile(base + d))
                some_key_ahead = d * t + t - 1 > r0 - edge
                masks.append(visible(cmr, r0 - d * t) if some_key_ahead else None)
            step(slice(r0, r0 + chunk), _Keys(starts, masks, r0 - edge))

    groups = base // UNROLL

    def body(g, carry):
        newest = base - 1 - g * UNROLL
        step(slice(0, rows), _Keys([tile(newest - u) for u in range(UNROLL)], [None] * UNROLL))
        return carry

    lax.fori_loop(0, groups, body, 0)
    if per_block % UNROLL == 0:
        return
    left = base - groups * UNROLL
    for r in range(1, UNROLL):
        @pl.when(left == r)
        def _():
            step(slice(0, rows), _Keys([tile(r - 1 - u) for u in range(r)], [None] * r))


def _sb_kernel(q_ref, k_ref, v_ref, o_ref, carry_ref, acc_ref, *, tq):
    t = ATT_TILE
    i = pl.program_id(1)
    rows = q_ref.shape[0]
    r = lax.broadcasted_iota(jnp.int32, (t, t), 0)
    c = lax.broadcasted_iota(jnp.int32, (t, t), 1)
    suffix = (r > c).astype(BF16)

    def step(rs, keys):
        q = q_ref[rs, :]
        carry = carry_ref[rs, :]
        pv = None
        for start, mask in zip(keys.starts, keys.masks):
            z = _dot_nt(q, k_ref[pl.ds(start, t), :])
            sp = jnp.maximum(z, 0.0) + jnp.log(1.0 + jnp.exp2(jnp.abs(z) * (-LOG2_E)))
            if mask is not None:
                sp = jnp.where(mask, sp, 0.0)
            later = _dot(sp.astype(BF16), suffix)
            a = jnp.exp2((z - sp - later - carry) * LOG2_E)
            if mask is not None:
                a = jnp.where(mask, a, 0.0)
            d = _dot(a.astype(BF16), v_ref[pl.ds(start, t), :])
            pv = d if pv is None else pv + d
            carry = carry + (later[:, 0:1] + sp[:, 0:1])
        acc_ref[rs, :] += pv
        carry_ref[rs, :] = carry

    carry_ref[...] = jnp.zeros_like(carry_ref)
    acc_ref[...] = jnp.zeros_like(acc_ref)
    _causal_sweep(step, i, rows, tq, True)
    o_ref[...] = acc_ref[...].astype(o_ref.dtype)


def _q_block(seq):
    return min(8 * ATT_TILE, seq)


def _head_block(rows, width):
    if rows is None:
        return lambda seq: pl.BlockSpec((None, None, seq, width),
                                        lambda bh, i: (bh // N_HEADS, bh % N_HEADS, 0, 0))
    return pl.BlockSpec((None, None, rows, width), lambda bh, i: (bh // N_HEADS, bh % N_HEADS, i, 0))


def _stick_breaking(q, k, v, batch, seq):
    tq = _q_block(seq)
    return pl.pallas_call(
        functools.partial(_sb_kernel, tq=tq),
        grid=(batch * N_HEADS, seq // tq),
        in_specs=[_head_block(tq, HEAD_DIM), _head_block(None, HEAD_DIM)(seq), _head_block(None, HEAD_DIM)(seq)],
        out_specs=_head_block(tq, HEAD_DIM),
        out_shape=jax.ShapeDtypeStruct((batch, N_HEADS, seq, HEAD_DIM), BF16),
        scratch_shapes=[pltpu.VMEM((tq, 1), F32), pltpu.VMEM((tq, HEAD_DIM), F32)],
        compiler_params=_params("parallel", "arbitrary"),
        name="stick_breaking",
    )(q, k, v)


def _softmax_step(q, k_ref, v_ref, keys, rs, m_ref, acc_ref):
    t = ATT_TILE
    if m_ref is None:
        width = t * len(keys.starts)
        oldest = keys.starts[-1]
        s = _dot_nt(q, k_ref[pl.ds(oldest, width), :])
        if keys.limit is not None:
            cmr = _col_minus_row(q.shape[0], keys.row_mod, width)
            seen = cmr <= keys.limit
            if keys.floor is not None:
                seen = seen & (cmr > keys.floor)
            s = jnp.where(seen, s, NEG_INF)
        acc_ref[rs, :] += _dot(jnp.exp2(s).astype(BF16), v_ref[pl.ds(oldest, width), :])
        return
    for start, mask in reversed(list(zip(keys.starts, keys.masks))):
        s = _dot_nt(q, k_ref[pl.ds(start, t), :])
        if mask is not None:
            s = jnp.where(mask, s, NEG_INF)
        v1 = v_ref[pl.ds(start, t), :]
        m_old = m_ref[rs, :]
        mn = jnp.maximum(m_old, jnp.max(s, axis=1, keepdims=True))
        acc_ref[rs, :] = jnp.exp2(m_old - mn) * acc_ref[rs, :] + _dot(jnp.exp2(s - mn).astype(BF16), v1)
        m_ref[rs, :] = mn


def _either_path(bound_ref, m_refs, sweep):
    bounded = bound_ref[0] <= EXP2_LIMIT

    @pl.when(bounded)
    def _():
        sweep((None,) * len(m_refs))

    @pl.when(jnp.logical_not(bounded))
    def _():
        for ref in m_refs:
            ref[...] = jnp.full(ref.shape, NEG_INF, F32)
        sweep(m_refs)


def _normalised(acc):
    return acc[:, :HEAD_DIM] / acc[:, HEAD_DIM:HEAD_DIM + 1]


def _diff_kernel(bound_ref, q_ref, k_ref, v_ref, lam_ref, g_ref, o_ref, q1_ref, q2_ref, m1_ref, m2_ref,
                 a1_ref, a2_ref, *, lam_init, tq):
    i = pl.program_id(1)
    rows = q_ref.shape[0]
    q = q_ref[...]
    lane = lax.broadcasted_iota(jnp.int32, q.shape, 1)
    zero = jnp.zeros_like(q)
    q1_ref[...] = jnp.where(lane < DIFF_DIM, q, zero)
    q2_ref[...] = jnp.where(lane >= DIFF_DIM, q, zero)
    for ref in (a1_ref, a2_ref):
        ref[...] = jnp.zeros_like(ref)

    def sweep(m_refs):
        def step(rs, keys):
            _softmax_step(q1_ref[rs, :], k_ref, v_ref, keys, rs, m_refs[0], a1_ref)
            _softmax_step(q2_ref[rs, :], k_ref, v_ref, keys, rs, m_refs[1], a2_ref)

        _causal_sweep(step, i, rows, tq, False)

    _either_path(bound_ref, (m1_ref, m2_ref), sweep)

    lp = lam_ref[...]
    lam = (jnp.exp(jnp.sum(lp[0:1, :] * lp[1:2, :], axis=1, keepdims=True))
           - jnp.exp(jnp.sum(lp[2:3, :] * lp[3:4, :], axis=1, keepdims=True)) + lam_init)
    o = _normalised(a1_ref[...]) - lam * _normalised(a2_ref[...])
    o = o * lax.rsqrt(jnp.mean(o * o, axis=-1, keepdims=True) + RMS_EPS) * g_ref[...]
    o_ref[...] = (o * (1.0 - lam_init)).astype(o_ref.dtype)


_SCALAR = pl.BlockSpec(memory_space=pltpu.SMEM)


def _diff_attention(bound, q, k, v1, lam_p, subln_g, batch, seq, lam_init):
    tq = _q_block(seq)
    fix = lambda bh, i: (0, 0)
    return pl.pallas_call(
        functools.partial(_diff_kernel, lam_init=lam_init, tq=tq),
        grid=(batch * N_HEADS, seq // tq),
        in_specs=[_SCALAR, _head_block(tq, HEAD_DIM), _head_block(None, HEAD_DIM)(seq),
                  _head_block(None, LANES)(seq),
                  pl.BlockSpec((4, DIFF_DIM), fix), pl.BlockSpec((1, HEAD_DIM), fix)],
        out_specs=_head_block(tq, HEAD_DIM),
        out_shape=jax.ShapeDtypeStruct((batch, N_HEADS, seq, HEAD_DIM), BF16),
        scratch_shapes=[pltpu.VMEM((tq, HEAD_DIM), BF16), pltpu.VMEM((tq, HEAD_DIM), BF16),
                        pltpu.VMEM((tq, 1), F32), pltpu.VMEM((tq, 1), F32),
                        pltpu.VMEM((tq, LANES), F32), pltpu.VMEM((tq, LANES), F32)],
        compiler_params=_params("parallel", "arbitrary"),
        name="diff_attention",
    )(bound, q, k, v1, lam_p, subln_g)


def _select_bias(score_t, keep_extra_t, ntop):
    nblk = score_t.shape[0]
    sub = lax.broadcasted_iota(jnp.int32, score_t.shape, 0)
    rank = jnp.zeros(score_t.shape, jnp.int32)
    for b in range(nblk):
        row = score_t[b:b + 1, :]
        ahead = (row > score_t) | ((row == score_t) & (sub > b))
        rank = rank + ahead.astype(jnp.int32)
    keep = rank < ntop
    if keep_extra_t is not None:
        keep = keep_extra_t(keep, sub)
    return jnp.where(keep, 0.0, NEG_INF)


def _bias_lanes(bias_t, lane0):
    nblk, t = bias_t.shape
    parts = []
    if lane0:
        parts.append(jnp.zeros((lane0, t), F32))
    parts.append(bias_t)
    if LANES - lane0 - nblk:
        parts.append(jnp.zeros((LANES - lane0 - nblk, t), F32))
    return jnp.concatenate(parts, axis=0).T


def _moba_gate_kernel(q_ref, k_ref, a_ref, o_ref, km_hi, km_lo, *, nblk):
    i = pl.program_id(1)

    @pl.when(i == 0)
    def _():
        kmean = _dot(a_ref[...], k_ref[...])
        hi, lo = _split_bf16(kmean)
        km_hi[...] = hi
        km_lo[...] = lo

    q = q_ref[...]
    rows = q.shape[0]
    gate = _dot_nt(q, km_hi[...]) + _dot_nt(q, km_lo[...])
    gate_t = gate.T[HEAD_DIM:HEAD_DIM + nblk, :]
    blk = lax.broadcasted_iota(jnp.int32, gate_t.shape, 0)
    own = (i * rows + lax.broadcasted_iota(jnp.int32, gate_t.shape, 1)) // MOBA_BLOCK
    score_t = jnp.where(blk < own, gate_t, NEG_INF)
    ksel = min(MOBA_TOPK, nblk - 1)

    def keep_rule(keep, sub):
        return (keep & (sub < own)) | (sub == own)

    bias_t = _select_bias(score_t, keep_rule, ksel)
    o_ref[...] = (q.astype(F32) + _bias_lanes(bias_t, HEAD_DIM)).astype(o_ref.dtype)


def _moba_gate(q_pad, k_aug, amat, batch, seq):
    tq = _q_block(seq)
    nblk = seq // MOBA_BLOCK
    return pl.pallas_call(
        functools.partial(_moba_gate_kernel, nblk=nblk),
        grid=(batch * N_HEADS, seq // tq),
        in_specs=[_head_block(tq, LANES), _head_block(None, LANES)(seq),
                  pl.BlockSpec((LANES, seq), lambda bh, i: (0, 0))],
        out_specs=_head_block(tq, LANES),
        out_shape=jax.ShapeDtypeStruct((batch, N_HEADS, seq, LANES), BF16),
        scratch_shapes=[pltpu.VMEM((LANES, LANES), BF16), pltpu.VMEM((LANES, LANES), BF16)],
        compiler_params=_params("parallel", "arbitrary"),
        name="moba_gate",
    )(q_pad, k_aug, amat)


def _flash_kernel(bound_ref, q_ref, k_ref, v_ref, o_ref, m_ref, acc_ref, *, tq, window_tiles):
    t = ATT_TILE
    i = pl.program_id(1)
    rows = q_ref.shape[0]
    acc_ref[...] = jnp.zeros_like(acc_ref)

    def sweep(m_refs):
        def step(rs, keys):
            _softmax_step(q_ref[rs, :], k_ref, v_ref, keys, rs, m_refs[0], acc_ref)

        if window_tiles is None:
            _causal_sweep(step, i, rows, tq, False)
            return
        assert tq == t
        everything = slice(0, rows)
        tile = lambda back: pl.multiple_of((i - back) * t, t)
        cmr = _col_minus_row(rows, tq)

        @pl.when(i >= window_tiles)
        def _():
            top = window_tiles * t
            step(everything, _Keys([tile(b) for b in range(window_tiles + 1)],
                                   [cmr <= 0] + [None] * (window_tiles - 1) + [cmr > top - NSA_WINDOW],
                                   top, tq, floor=top - NSA_WINDOW))

        @pl.when(i < window_tiles)
        def _():
            step(everything, _Keys([tile(0)], [cmr <= 0], 0, tq))
            for back in range(1, window_tiles):
                @pl.when(i >= back)
                def _():
                    step(everything, _Keys([tile(back)], [None]))

    _either_path(bound_ref, (m_ref,), sweep)
    o_ref[...] = _normalised(acc_ref[...]).astype(o_ref.dtype)


def _flash(bound, q, k, v1, *, q_map, kv_map, o_map, grid, rows, tq, seq, out_lead, out_dtype, window_tiles,
           name):
    lead = (None,) * (q.ndim - 2)
    klead = (None,) * (k.ndim - 2)
    return pl.pallas_call(
        functools.partial(_flash_kernel, tq=tq, window_tiles=window_tiles),
        grid=grid,
        in_specs=[_SCALAR, pl.BlockSpec(lead + (rows, LANES), q_map),
                  pl.BlockSpec(klead + (seq, LANES), kv_map),
                  pl.BlockSpec(klead + (seq, LANES), kv_map)],
        out_specs=pl.BlockSpec(lead + (rows, HEAD_DIM), o_map),
        out_shape=jax.ShapeDtypeStruct(out_lead + (HEAD_DIM,), out_dtype),
        scratch_shapes=[pltpu.VMEM((rows, 1), F32), pltpu.VMEM((rows, LANES), F32)],
        compiler_params=_params("parallel", "arbitrary"),
        name=name,
    )(bound, q, k, v1)


def _nsa_compress_kernel(tk_ref, tv_ref, pe_ref, wk_ref, wv_ref, g_ref, cos_ref, sa_ref, sb_ref,
                         kc_ref, vc_ref):
    rows = tk_ref.shape[0]

    def compress(tok, pe_lo, pe_hi, w_ref):
        half = w_ref.shape[0] // 2
        a = _dot((tok + pe_lo).astype(BF16), w_ref[0:half, :])
        b = _dot((tok + pe_hi).astype(BF16), w_ref[half:, :])
        return a + pltpu.roll(b, rows - 1, 0)

    kc = compress(tk_ref[...].astype(F32), pe_ref[0:1, :], pe_ref[1:2, :], wk_ref)
    vc = compress(tv_ref[...].astype(F32), pe_ref[2:3, :], pe_ref[3:4, :], wv_ref)
    ms = jnp.sum(kc * kc, axis=-1, keepdims=True) / HEAD_DIM
    kc = kc * lax.rsqrt(ms + RMS_EPS) * g_ref[...]
    kc = _rope_apply(kc, cos_ref[...], sa_ref[...], sb_ref[...], HEAD_DIM // ROPE_FRACTION_DEN // 2)
    kc_ref[...] = kc.astype(kc_ref.dtype)
    vc_ref[...] = vc.astype(vc_ref.dtype)


def _nsa_compress(tk, tv, pe4, wk, wv, gk, tabs, batch):
    rows, width = tk.shape[1], tk.shape[2]
    b3 = lambda b: (b, 0, 0)
    fix = lambda b: (0, 0)
    tok = pl.BlockSpec((None, rows, width), b3)
    out = pl.BlockSpec((None, rows, LANES), b3)
    tab = pl.BlockSpec((rows, LANES), fix)
    return pl.pallas_call(
        _nsa_compress_kernel,
        grid=(batch,),
        in_specs=[tok, tok, pl.BlockSpec((4, width), fix),
                  pl.BlockSpec((2 * width, LANES), fix), pl.BlockSpec((2 * width, LANES), fix),
                  pl.BlockSpec((1, LANES), fix), tab, tab, tab],
        out_specs=[out, out],
        out_shape=[jax.ShapeDtypeStruct((batch, rows, LANES), BF16)] * 2,
        compiler_params=_params("parallel"),
        name="nsa_compress",
    )(tk, tv, pe4, wk, wv, gk, *tabs)


def _nsa_select_kernel(q_ref, kc_ref, vc_ref, m_ref, qa_ref, oc_ref, *, nslc, ntop):
    t = ATT_TILE
    i = pl.program_id(1)
    q = q_ref[...]
    rows = q.shape[0]
    ncmp = kc_ref.shape[0]
    s = _dot_nt(q, kc_ref[...])
    qpos = i * t + (lax.broadcasted_iota(jnp.int32, (rows, ncmp), 0) & (t - 1))
    cmp_end = NSA_CMP_STRIDE * lax.broadcasted_iota(jnp.int32, (rows, ncmp), 1) + (NSA_CMP_LEN - 1)
    mask = cmp_end <= qpos
    s = jnp.where(mask, s, NEG_INF)
    e = jnp.where(mask, jnp.exp2(s - jnp.max(s, axis=1, keepdims=True)), 0.0)
    p = e / jnp.maximum(jnp.sum(e, axis=1, keepdims=True), 1e-30)
    oc_ref[...] = _dot(p.astype(BF16), vc_ref[...])[:, :HEAD_DIM].astype(oc_ref.dtype)

    psum = p[0:t]
    for h in range(1, rows // t):
        psum = psum + p[h * t:(h + 1) * t]
    hi, lo = _split_bf16(psum)
    imp = _dot(hi, m_ref[...]) + _dot(lo, m_ref[...])
    imp_t = imp.T[HEAD_DIM:HEAD_DIM + nslc, :]
    blk = lax.broadcasted_iota(jnp.int32, imp_t.shape, 0)
    cur = (i * t + lax.broadcasted_iota(jnp.int32, imp_t.shape, 1)) // NSA_SLC_BLOCK
    forced = (blk == 0) | (blk == cur) | (blk == cur - 1)
    allowed = blk <= cur
    score_t = jnp.where(allowed, jnp.where(forced, BIG, imp_t), NEG_INF)
    bias = _bias_lanes(_select_bias(score_t, None, ntop), HEAD_DIM)
    qf = q.astype(F32)
    for h in range(rows // t):
        qa_ref[h * t:(h + 1) * t, :] = (qf[h * t:(h + 1) * t] + bias).astype(qa_ref.dtype)


def _nsa_select(q4, kc, vc, mmat, batch, seq):
    t = ATT_TILE
    rows = N_HEADS * t
    ncmp = kc.shape[1]
    nslc = seq // NSA_SLC_BLOCK
    qmap = lambda b, i: (b, i, 0, 0)
    kmap = lambda b, i: (b, 0, 0)
    return pl.pallas_call(
        functools.partial(_nsa_select_kernel, nslc=nslc, ntop=min(NSA_TOPN, nslc)),
        grid=(batch, seq // t),
        in_specs=[pl.BlockSpec((None, None, rows, LANES), qmap),
                  pl.BlockSpec((None, ncmp, LANES), kmap), pl.BlockSpec((None, ncmp, LANES), kmap),
                  pl.BlockSpec((ncmp, LANES), lambda b, i: (0, 0))],
        out_specs=[pl.BlockSpec((None, None, rows, LANES), qmap),
                   pl.BlockSpec((None, None, rows, HEAD_DIM), qmap)],
        out_shape=[jax.ShapeDtypeStruct((batch, seq // t, rows, LANES), BF16),
                   jax.ShapeDtypeStruct((batch, seq // t, rows, HEAD_DIM), F32)],
        compiler_params=_params("parallel", "arbitrary"),
        name="nsa_select",
    )(q4, kc, vc, mmat)


def _merge_kernel(x_ref, g_ref, oa_ref, ob_ref, oc_ref, dc_ref, ds_ref, dw_ref, gl_ref, e_ref,
                  wg_ref, bg_ref, wb_ref, wo_ref, y_ref):
    t = ATT_TILE
    x = x_ref[...]
    xn = x * lax.rsqrt(jnp.mean(x * x, axis=-1, keepdims=True) + RMS_EPS) * g_ref[...]
    xn = xn.astype(BF16)

    def token_major(ref):
        return jnp.concatenate([ref[h] for h in range(N_HEADS)], axis=1)

    def unstack(ref):
        return jnp.concatenate(
            [jnp.concatenate([ref[qt, h * t:(h + 1) * t, :] for h in range(N_HEADS)], axis=1)
             for qt in range(ref.shape[0])], axis=0)

    sig = 1.0 / (1.0 + jnp.exp(-gl_ref[...].astype(F32)))
    hi, lo = _split_bf16(sig)
    gexp = _dot(hi, e_ref[...]) + _dot(lo, e_ref[...])
    w = MIX_WIDTH
    o_d = (gexp[:, 0:w] * unstack(dc_ref) + gexp[:, w:2 * w] * unstack(ds_ref)
           + gexp[:, 2 * w:3 * w] * unstack(dw_ref)).astype(BF16)
    merged = None
    for bi, o in enumerate((token_major(oa_ref), token_major(ob_ref), token_major(oc_ref), o_d)):
        gate = 1.0 / (1.0 + jnp.exp(-(_dot(xn, wg_ref[bi]) + bg_ref[bi])))
        term = gate * _dot(o, wb_ref[bi])
        merged = term if merged is None else merged + term
    y_ref[...] = x + _dot(merged.astype(BF16), wo_ref[...])


def _merge(x2, g, oa, ob, oc, dc, ds, dw, glog, emat, wg, bg, wb, wo, seq, tm):
    n, d = x2.shape
    nst = seq // tm
    t = ATT_TILE
    row = lambda i: (i, 0)
    fix2 = lambda i: (0, 0)
    fix3 = lambda i: (0, 0, 0)
    heads = pl.BlockSpec((None, N_HEADS, tm, HEAD_DIM), lambda i: (i // nst, 0, i % nst, 0))
    stacked = pl.BlockSpec((None, tm // t, N_HEADS * t, HEAD_DIM), lambda i: (i // nst, i % nst, 0, 0))
    return pl.pallas_call(
        _merge_kernel,
        grid=(n // tm,),
        in_specs=[pl.BlockSpec((tm, d), row), _resident((1, d), fix2), heads, heads, heads,
                  stacked, stacked, stacked,
                  pl.BlockSpec((None, tm, LANES), lambda i: (i // nst, i % nst, 0)),
                  _resident((LANES, 3 * MIX_WIDTH), fix2),
                  _resident((4, d, d), fix3), _resident((4, 1, d), fix3),
                  _resident((4, MIX_WIDTH, d), fix3), _resident((d, d), fix2)],
        out_specs=pl.BlockSpec((tm, d), row),
        out_shape=jax.ShapeDtypeStruct((n, d), F32),
        compiler_params=_params("parallel"),
        name="gated_merge",
    )(x2, g, oa, ob, oc, dc, ds, dw, glog, emat, wg, bg, wb, wo)


def _ffn_kernel(x_ref, g_ref, wg_ref, wu_ref, wd_ref, y_ref):
    x = x_ref[...]
    hn = x * lax.rsqrt(jnp.mean(x * x, axis=-1, keepdims=True) + RMS_EPS) * g_ref[...]
    hn = hn.astype(BF16)
    gate = _dot(hn, wg_ref[...])
    up = _dot(hn, wu_ref[...])
    act = gate * (1.0 / (1.0 + jnp.exp(-gate))) * up
    y_ref[...] = x + _dot(act.astype(BF16), wd_ref[...])


def _ffn(x2, g, wg, wu, wd, tm):
    n, d = x2.shape
    f = wg.shape[1]
    row = lambda i: (i, 0)
    fix = lambda i: (0, 0)
    return pl.pallas_call(
        _ffn_kernel,
        grid=(n // tm,),
        in_specs=[pl.BlockSpec((tm, d), row), _resident((1, d), fix), _resident((d, f), fix),
                  _resident((d, f), fix), _resident((f, d), fix)],
        out_specs=pl.BlockSpec((tm, d), row),
        out_shape=jax.ShapeDtypeStruct((n, d), F32),
        compiler_params=_params("parallel"),
        name="swiglu",
    )(x2, g, wg, wu, wd)


def _layer_constants(seq):
    t = ATT_TILE
    pos = jnp.arange(seq, dtype=jnp.int32)
    tabs64 = _rope_tables(pos, HEAD_DIM, GROUP)
    tabs32 = _rope_tables(pos, DIFF_DIM, GROUP)
    ncmp = seq // NSA_CMP_STRIDE
    cmp_end = NSA_CMP_STRIDE * jnp.arange(ncmp, dtype=jnp.int32) + (NSA_CMP_LEN - 1)
    tabs_cmp = _rope_tables(cmp_end, HEAD_DIM, HEAD_DIM)
    tabs_cmp = tuple(jnp.pad(a, ((0, 0), (0, LANES - HEAD_DIM)), constant_values=c)
                     for a, c in zip(tabs_cmp, (1.0, 0.0, 0.0)))

    s = np.arange(seq)
    nblk = seq // MOBA_BLOCK
    nslc = seq // NSA_SLC_BLOCK
    assert nblk <= LANES - HEAD_DIM and nslc <= LANES - HEAD_DIM and seq % t == 0
    moba_hot = np.zeros((seq, LANES - HEAD_DIM), np.float32)
    moba_hot[s, s // MOBA_BLOCK] = 1.0
    slc_hot = np.zeros((seq, LANES - HEAD_DIM), np.float32)
    slc_hot[s, s // NSA_SLC_BLOCK] = 1.0
    amat = np.zeros((LANES, seq), np.float32)
    amat[HEAD_DIM + s // MOBA_BLOCK, s] = 1.0 / MOBA_BLOCK

    nc_real = (seq - NSA_CMP_LEN) // NSA_CMP_STRIDE + 1
    cstart = NSA_CMP_STRIDE * np.arange(ncmp)
    sstart = NSA_SLC_BLOCK * np.arange(nslc)
    overlap = np.clip(np.minimum(cstart[:, None] + NSA_CMP_LEN, sstart[None, :] + NSA_SLC_BLOCK)
                      - np.maximum(cstart[:, None], sstart[None, :]), 0, None)
    mmat = np.zeros((ncmp, LANES), np.float32)
    mmat[:nc_real, HEAD_DIM:HEAD_DIM + nslc] = overlap[:nc_real].astype(np.float32) / NSA_CMP_STRIDE

    emat = np.zeros((LANES, 3 * MIX_WIDTH), np.float32)
    for br in range(3):
        for hd in range(N_HEADS):
            emat[br * N_HEADS + hd,
                 br * MIX_WIDTH + hd * HEAD_DIM: br * MIX_WIDTH + (hd + 1) * HEAD_DIM] = 1.0

    return dict(tabs64=tabs64, tabs32=tabs32, tabs_cmp=tabs_cmp,
                moba_hot=jnp.asarray(moba_hot, BF16), slc_hot=jnp.asarray(slc_hot, BF16),
                amat=jnp.asarray(amat, BF16), mmat=jnp.asarray(mmat, BF16), emat=jnp.asarray(emat, BF16))


def _lane_vectors(diff_qn_g, diff_kn_g, moba_qn_g, moba_kn_g, nsa_qn_g, nsa_kn_g):
    ones = jnp.ones((GROUP,), F32)
    zeros = jnp.zeros((GROUP,), F32)
    sc64_log2 = LOG2_E / math.sqrt(HEAD_DIM)
    sc32_log2 = LOG2_E / math.sqrt(DIFF_DIM)
    quarter = lambda g, k: jnp.concatenate([g if j == k else jnp.ones((HEAD_DIM,), F32) for j in range(4)])
    flag = lambda k: jnp.concatenate([jnp.full((HEAD_DIM,), 1.0 if j == k else 0.0, F32) for j in range(4)])
    gains = [ones / math.sqrt(HEAD_DIM), ones, ones,
             jnp.tile(diff_qn_g, GROUP // DIFF_DIM) * sc32_log2, jnp.tile(diff_kn_g, GROUP // DIFF_DIM), ones,
             jnp.tile(moba_qn_g, N_HEADS) * sc64_log2, jnp.tile(moba_kn_g, N_HEADS), ones,
             jnp.tile(nsa_qn_g, N_HEADS) * sc64_log2,
             quarter(nsa_kn_g[1], 2), quarter(nsa_kn_g[2], 0)]
    flags = [zeros, zeros, zeros, ones, ones, zeros, ones, ones, zeros, ones, flag(2), flag(0)]
    shape = (N_GROUPS, 1, GROUP)
    return jnp.stack(gains).reshape(shape), jnp.stack(flags).reshape(shape), jnp.stack(flags).reshape(shape)


def _row_tile(n, seq, want):
    tm = want if (n % want == 0 and seq % want == 0) else ATT_TILE
    return min(tm, seq)


def _mixers(x2, lw, consts, batch, seq, lam_init):
    n, d = x2.shape
    t = ATT_TILE
    nq = seq // t
    tm = _row_tile(n, seq, 512)

    gain, nflag, rflag = _lane_vectors(lw["diff_qn_g"], lw["diff_kn_g"], lw["moba_qn_g"],
                                       lw["moba_kn_g"], lw["nsa_qn_g"], lw["nsa_kn_g"])
    w_in = jnp.pad(lw["w_in"], ((0, 0), (0, N_GROUPS * GROUP - lw["w_in"].shape[1]))).astype(BF16)
    (sbq, sbk, sbv, dfq, dfk, dfv1, mbq, mbk, mbv1, nsq, kct, vct, ksa, vs1, kwp, vw1, glog) = _inproj(
        x2, lw["attn_norm_g"].reshape(1, d), w_in, gain, nflag, rflag, consts, batch, seq, tm)

    amax = lambda g: jnp.max(jnp.abs(g))
    score_bound = lambda d_head, gq, gk: (math.sqrt(d_head) * LOG2_E * NORM_SLACK * NORM_SLACK
                                          * amax(gq) * amax(gk)).reshape(1).astype(F32)

    o_a = _stick_breaking(sbq, sbk, sbv, batch, seq)
    o_b = _diff_attention(score_bound(DIFF_DIM, lw["diff_qn_g"], lw["diff_kn_g"]), dfq, dfk, dfv1,
                          lw["diff_lam"], lw["diff_subln_g"].reshape(1, HEAD_DIM), batch, seq, lam_init)

    mq_aug = _moba_gate(mbq, mbk, consts["amat"], batch, seq)
    bh_map = lambda bh, i: (bh // N_HEADS, bh % N_HEADS, i, 0)
    bh_full = lambda bh, i: (bh // N_HEADS, bh % N_HEADS, 0, 0)
    tq = _q_block(seq)
    o_c = _flash(score_bound(HEAD_DIM, lw["moba_qn_g"], lw["moba_kn_g"]), mq_aug, mbk, mbv1,
                 q_map=bh_map, kv_map=bh_full, o_map=bh_map,
                 grid=(batch * N_HEADS, seq // tq), rows=tq, tq=tq, seq=seq,
                 out_lead=(batch, N_HEADS, seq), out_dtype=BF16, window_tiles=None, name="moba_attention")

    ncmp = seq // NSA_CMP_STRIDE
    tok = lambda a: a.reshape(batch, ncmp, NSA_CMP_STRIDE * HEAD_DIM)
    pad64 = lambda a: jnp.pad(a, [(0, 0)] * (a.ndim - 1) + [(0, LANES - HEAD_DIM)])
    pe4 = lw["nsa_cmp_pe"].reshape(4, NSA_CMP_STRIDE * HEAD_DIM)
    wc = pad64(lw["nsa_cmp_w"]).astype(BF16)
    gk = pad64(lw["nsa_kn_g"][0].reshape(1, HEAD_DIM))
    kc, vc = _nsa_compress(tok(kct), tok(vct), pe4, wc[0], wc[1], gk, consts["tabs_cmp"], batch)
    q_aug, d_c = _nsa_select(nsq, kc, vc, consts["mmat"], batch, seq)

    rows = N_HEADS * t
    b_map = lambda b, i: (b, i, 0, 0)
    b_full = lambda b, i: (b, 0, 0)
    nsa_flash = functools.partial(_flash, q_map=b_map, kv_map=b_full, o_map=b_map, grid=(batch, nq),
                                  rows=rows, tq=t, seq=seq, out_lead=(batch, nq, rows), out_dtype=F32)
    d_s = nsa_flash(score_bound(HEAD_DIM, lw["nsa_qn_g"], lw["nsa_kn_g"][1]), q_aug, ksa, vs1,
                    window_tiles=None, name="nsa_selected")
    d_w = nsa_flash(score_bound(HEAD_DIM, lw["nsa_qn_g"], lw["nsa_kn_g"][2]), nsq, kwp, vw1,
                    window_tiles=NSA_WINDOW // t, name="nsa_window")
    return glog, o_a, o_b, o_c, d_c, d_s, d_w


def _layer(x2, lw, consts, batch, seq, lam_init):
    n, d = x2.shape
    glog, o_a, o_b, o_c, d_c, d_s, d_w = _mixers(x2, lw, consts, batch, seq, lam_init)
    x2 = _merge(x2, lw["attn_norm_g"].reshape(1, d), o_a, o_b, o_c, d_c, d_s, d_w,
                glog, consts["emat"], lw["w_gate"].astype(BF16), lw["b_gate"].reshape(4, 1, d),
                lw["w_branch"].astype(BF16), lw["w_out"].astype(BF16), seq, _row_tile(n, seq, 512))
    return _ffn(x2, lw["ffn_norm_g"].reshape(1, d), lw["w_ffn_gate"].astype(BF16),
                lw["w_ffn_up"].astype(BF16), lw["w_ffn_down"].astype(BF16), _row_tile(n, seq, 256))


def kernel(x, attn_norm_g, w_in, diff_qn_g, diff_kn_g, diff_lam, diff_subln_g, moba_qn_g, moba_kn_g,
           nsa_qn_g, nsa_kn_g, nsa_cmp_pe, nsa_cmp_w, w_gate, b_gate, w_branch, w_out, ffn_norm_g,
           w_ffn_gate, w_ffn_up, w_ffn_down):
    batch, seq, d = x.shape
    weights = dict(attn_norm_g=attn_norm_g, w_in=w_in, diff_qn_g=diff_qn_g, diff_kn_g=diff_kn_g,
                   diff_lam=diff_lam, diff_subln_g=diff_subln_g, moba_qn_g=moba_qn_g, moba_kn_g=moba_kn_g,
                   nsa_qn_g=nsa_qn_g, nsa_kn_g=nsa_kn_g, nsa_cmp_pe=nsa_cmp_pe, nsa_cmp_w=nsa_cmp_w,
                   w_gate=w_gate, b_gate=b_gate, w_branch=w_branch, w_out=w_out, ffn_norm_g=ffn_norm_g,
                   w_ffn_gate=w_ffn_gate, w_ffn_up=w_ffn_up, w_ffn_down=w_ffn_down)
    consts = _layer_constants(seq)
    x2 = x.reshape(batch * seq, d)
    for layer in range(w_in.shape[0]):
        lw = {k: v[layer] for k, v in weights.items()}
        lam_init = 0.8 - 0.6 * math.exp(-0.3 * layer)
        x2 = _layer(x2, lw, consts, batch, seq, lam_init)
    return x2.reshape(batch, seq, d)
```

```python
import functools
import math
from typing import Any, NamedTuple

import numpy as np
import jax
import jax.numpy as jnp
from jax import lax
from jax.experimental import pallas as pl
from jax.experimental.pallas import tpu as pltpu

HEAD_DIM = 64
N_HEADS = 4
MIX_WIDTH = N_HEADS * HEAD_DIM
ROPE_THETA = 500000.0
ROPE_FRACTION_DEN = 4
DIFF_DIM = HEAD_DIM // 2
MOBA_BLOCK = 256
MOBA_TOPK = 3
NSA_CMP_LEN = 32
NSA_CMP_STRIDE = 16
NSA_SLC_BLOCK = 64
NSA_TOPN = 16
NSA_WINDOW = 512
NEG_INF = -1e30
BIG = 1e30
RMS_EPS = 1e-6

LANES = 128
ATT_TILE = 256
UNROLL = 4
DIAG_ROWS = 512
GROUP = 256
N_GROUPS = 12
VMEM_LIMIT = 56 * 1024 * 1024
LOG2_E = math.log2(math.e)
EXP2_LIMIT = 40.0
NORM_SLACK = 1.05

F32 = jnp.float32
BF16 = jnp.bfloat16

_NT = (((1,), (1,)), ((), ()))


def _dot(a, b):
    return jnp.dot(a, b, preferred_element_type=F32)


def _dot_nt(a, b):
    return lax.dot_general(a, b, _NT, preferred_element_type=F32)


def _split_bf16(x):
    hi = x.astype(BF16)
    lo = (x - hi.astype(F32)).astype(BF16)
    return hi, lo


def _params(*sem):
    return pltpu.CompilerParams(dimension_semantics=sem, vmem_limit_bytes=VMEM_LIMIT)


def _resident(shape, index_map):
    return pl.BlockSpec(shape, index_map, pipeline_mode=pl.Buffered(1))


def _rope_tables(pos, d, width):
    r = d // ROPE_FRACTION_DEN
    half = r // 2
    inv = ROPE_THETA ** (-jnp.arange(half, dtype=F32) * 2.0 / r)
    ang = pos.astype(F32)[:, None] * inv[None, :]
    cos, sin = jnp.cos(ang), jnp.sin(ang)
    n = pos.shape[0]
    one = jnp.ones((n, d - r), F32)
    zero_h = jnp.zeros((n, half), F32)
    zero_t = jnp.zeros((n, d - r), F32)
    cos_g = jnp.concatenate([cos, cos, one], axis=1)
    sa_g = jnp.concatenate([-sin, zero_h, zero_t], axis=1)
    sb_g = jnp.concatenate([zero_h, sin, zero_t], axis=1)
    reps = width // d
    return (jnp.tile(cos_g, (1, reps)), jnp.tile(sa_g, (1, reps)), jnp.tile(sb_g, (1, reps)))


def _group_mean_matrix(d, width):
    g = np.arange(width) // d
    return jnp.asarray((g[:, None] == g[None, :]).astype(np.float32) / d, BF16)


def _rope_apply(y, cos, sa, sb, half):
    w = y.shape[-1]
    return y * cos + pltpu.roll(y, w - half, 1) * sa + pltpu.roll(y, half, 1) * sb


_GROUP_CFG = (
    (None, None), (None, None), (None, None),
    (DIFF_DIM, DIFF_DIM), (DIFF_DIM, DIFF_DIM), (None, None),
    (HEAD_DIM, HEAD_DIM), (HEAD_DIM, HEAD_DIM), (None, None),
    (HEAD_DIM, HEAD_DIM),
    (HEAD_DIM, HEAD_DIM),
    (HEAD_DIM, HEAD_DIM),
)


def _inproj_kernel(x_ref, g_ref, w_ref, gain_ref, nflag_ref, rflag_ref, b64_ref, b32_ref,
                   c64_ref, sa64_ref, sb64_ref, c32_ref, sa32_ref, sb32_ref, mhot_ref, shot_ref,
                   sbq, sbk, sbv, dfq, dfk, dfv, mbq, mbk, mbv, nsq, kct, vct, ksa, vs1, kwp, vw1, glog):
    x = x_ref[...]
    tm = x.shape[0]
    t = ATT_TILE
    hd = HEAD_DIM
    xn = x * lax.rsqrt(jnp.mean(x * x, axis=-1, keepdims=True) + RMS_EPS) * g_ref[...]
    xn = xn.astype(BF16)
    zeros = jnp.zeros((tm, hd), BF16)
    ones_lane = jnp.where(lax.broadcasted_iota(jnp.int32, (tm, hd), 1) == 0, 1.0, 0.0).astype(BF16)

    def project(gi):
        norm, rope = _GROUP_CFG[gi]
        y = _dot(xn, w_ref[:, gi * GROUP:(gi + 1) * GROUP])
        gain = gain_ref[gi]
        if norm is not None:
            bmat = b64_ref[...] if norm == HEAD_DIM else b32_ref[...]
            ms = _dot((y * y).astype(BF16), bmat)
            y = y * jnp.where(nflag_ref[gi] > 0, lax.rsqrt(ms + RMS_EPS) * gain, gain)
        else:
            y = y * gain
        if rope is not None:
            if rope == HEAD_DIM:
                cos, sa, sb = c64_ref[...], sa64_ref[...], sb64_ref[...]
            else:
                cos, sa, sb = c32_ref[...], sa32_ref[...], sb32_ref[...]
            rf = rflag_ref[gi]
            cos = jnp.where(rf > 0, cos, 1.0)
            y = _rope_apply(y, cos, sa * rf, sb * rf, rope // ROPE_FRACTION_DEN // 2)
        return y

    def quarters(y):
        return [y[:, h * hd:(h + 1) * hd].astype(BF16) for h in range(N_HEADS)]

    def put_heads(ref, gi, extra):
        for h, piece in enumerate(quarters(project(gi))):
            if extra is None:
                ref[h] = piece
            else:
                ref[h, :, 0:hd] = piece
                ref[h, :, hd:] = extra

    put_heads(sbq, 0, None)
    put_heads(sbk, 1, None)
    put_heads(sbv, 2, None)
    put_heads(dfq, 3, None)
    put_heads(dfk, 4, None)
    put_heads(dfv, 5, ones_lane)
    put_heads(mbq, 6, zeros)
    put_heads(mbk, 7, mhot_ref[...])
    put_heads(mbv, 8, ones_lane)

    for h, piece in enumerate(quarters(project(9))):
        for qt in range(tm // t):
            nsq[qt, h * t:(h + 1) * t, 0:hd] = piece[qt * t:(qt + 1) * t]
            nsq[qt, h * t:(h + 1) * t, hd:] = zeros[0:t]

    kc_tok, vc_tok, k_slc, v_slc = quarters(project(10))
    kct[...] = kc_tok
    vct[...] = vc_tok
    ksa[:, 0:hd] = k_slc
    ksa[:, hd:] = shot_ref[...]
    vs1[:, 0:hd] = v_slc
    vs1[:, hd:] = ones_lane

    y = project(11)
    kwp[:, 0:hd] = y[:, 0:hd].astype(BF16)
    kwp[:, hd:] = zeros
    vw1[:, 0:hd] = y[:, hd:2 * hd].astype(BF16)
    vw1[:, hd:] = ones_lane
    glog[...] = y[:, 2 * hd:].astype(BF16)


def _inproj(x2, g, w, gain, nflag, rflag, consts, batch, seq, tm):
    n, d = x2.shape
    ncols = N_GROUPS * GROUP
    nst = seq // tm
    t = ATT_TILE
    row = lambda i: (i, 0)
    fix2 = lambda i: (0, 0)
    fix3 = lambda i: (0, 0, 0)
    tab = lambda i: (i % nst, 0)
    vec = _resident((N_GROUPS, 1, GROUP), fix3)
    tspec = pl.BlockSpec((tm, GROUP), tab)
    hspec = pl.BlockSpec((tm, HEAD_DIM), tab)

    def heads_out(width):
        return (pl.BlockSpec((None, N_HEADS, tm, width), lambda i: (i // nst, 0, i % nst, 0)),
                jax.ShapeDtypeStruct((batch, N_HEADS, seq, width), BF16))

    def tokens_out(width):
        return (pl.BlockSpec((None, tm, width), lambda i: (i // nst, i % nst, 0)),
                jax.ShapeDtypeStruct((batch, seq, width), BF16))

    stacked_q = (pl.BlockSpec((None, tm // t, N_HEADS * t, LANES), lambda i: (i // nst, i % nst, 0, 0)),
                 jax.ShapeDtypeStruct((batch, seq // t, N_HEADS * t, LANES), BF16))
    outs = ([heads_out(HEAD_DIM)] * 5 + [heads_out(LANES)] * 4 + [stacked_q]
            + [tokens_out(HEAD_DIM)] * 2 + [tokens_out(LANES)] * 5)
    return pl.pallas_call(
        _inproj_kernel,
        grid=(n // tm,),
        in_specs=[pl.BlockSpec((tm, d), row), _resident((1, d), fix2), _resident((d, ncols), fix2),
                  vec, vec, vec, _resident((GROUP, GROUP), fix2), _resident((GROUP, GROUP), fix2),
                  tspec, tspec, tspec, tspec, tspec, tspec, hspec, hspec],
        out_specs=[o[0] for o in outs],
        out_shape=[o[1] for o in outs],
        compiler_params=_params("parallel"),
        name="inproj",
    )(x2, g, w, gain, nflag, rflag, _group_mean_matrix(HEAD_DIM, GROUP),
      _group_mean_matrix(DIFF_DIM, GROUP), *consts["tabs64"], *consts["tabs32"],
      consts["moba_hot"], consts["slc_hot"])


def _col_minus_row(nrows, row_mod=None, width=ATT_TILE):
    shape = (nrows, width)
    row = lax.broadcasted_iota(jnp.int32, shape, 0)
    if row_mod is not None:
        row = row & (row_mod - 1)
    return lax.broadcasted_iota(jnp.int32, shape, 1) - row


class _Keys(NamedTuple):
    starts: list
    masks: list
    limit: Any = None
    row_mod: Any = None
    floor: Any = None


def _causal_sweep(step, i, rows, tq, strict):
    t = ATT_TILE
    per_block = tq // t
    base = i * per_block
    tile = lambda j: pl.multiple_of(j * t, t)
    visible = (lambda cmr, lim: cmr < lim) if strict else (lambda cmr, lim: cmr <= lim)
    edge = 1 if strict else 0
    if tq == t:
        step(slice(0, rows), _Keys([tile(base)], [visible(_col_minus_row(rows, tq), 0)], -edge, tq))
    else:
        assert rows == tq
        chunk = min(DIAG_ROWS, rows)
        cmr = _col_minus_row(chunk)
        for r0 in range(0, rows, chunk):
            starts, masks = [], []
            for d in range((r0 + chunk - 1) // t, -1, -1):
                starts.append(tile(base + d))
                some_key_ahead = d * t + t - 1 > r0 - edge
                masks.append(visible(cmr, r0 - d * t) if some_key_ahead else None)
            step(slice(r0, r0 + chunk), _Keys(starts, masks, r0 - edge))

    groups = base // UNROLL

    def body(g, carry):
        newest = base - 1 - g * UNROLL
        step(slice(0, rows), _Keys([tile(newest - u) for u in range(UNROLL)], [None] * UNROLL))
        return carry

    lax.fori_loop(0, groups, body, 0)
    if per_block % UNROLL == 0:
        return
    left = base - groups * UNROLL
    for r in range(1, UNROLL):
        @pl.when(left == r)
        def _():
            step(slice(0, rows), _Keys([tile(r - 1 - u) for u in range(r)], [None] * r))


def _sb_kernel(q_ref, k_ref, v_ref, o_ref, carry_ref, acc_ref, *, tq):
    t = ATT_TILE
    i = pl.program_id(1)
    rows = q_ref.shape[0]
    r = lax.broadcasted_iota(jnp.int32, (t, t), 0)
    c = lax.broadcasted_iota(jnp.int32, (t, t), 1)
    suffix = (r > c).astype(BF16)

    def step(rs, keys):
        q = q_ref[rs, :]
        carry = carry_ref[rs, :]
        pv = None
        for start, mask in zip(keys.starts, keys.masks):
            z = _dot_nt(q, k_ref[pl.ds(start, t), :])
            sp = jnp.maximum(z, 0.0) + jnp.log(1.0 + jnp.exp2(jnp.abs(z) * (-LOG2_E)))
            if mask is not None:
                sp = jnp.where(mask, sp, 0.0)
            later = _dot(sp.astype(BF16), suffix)
            a = jnp.exp2((z - sp - later - carry) * LOG2_E)
            if mask is not None:
                a = jnp.where(mask, a, 0.0)
            d = _dot(a.astype(BF16), v_ref[pl.ds(start, t), :])
            pv = d if pv is None else pv + d
            carry = carry + (later[:, 0:1] + sp[:, 0:1])
        acc_ref[rs, :] += pv
        carry_ref[rs, :] = carry

    carry_ref[...] = jnp.zeros_like(carry_ref)
    acc_ref[...] = jnp.zeros_like(acc_ref)
    _causal_sweep(step, i, rows, tq, True)
    o_ref[...] = acc_ref[...].astype(o_ref.dtype)


def _q_block(seq):
    return min(8 * ATT_TILE, seq)


def _head_block(rows, width):
    if rows is None:
        return lambda seq: pl.BlockSpec((None, None, seq, width),
                                        lambda bh, i: (bh // N_HEADS, bh % N_HEADS, 0, 0))
    return pl.BlockSpec((None, None, rows, width), lambda bh, i: (bh // N_HEADS, bh % N_HEADS, i, 0))


def _stick_breaking(q, k, v, batch, seq):
    tq = _q_block(seq)
    return pl.pallas_call(
        functools.partial(_sb_kernel, tq=tq),
        grid=(batch * N_HEADS, seq // tq),
        in_specs=[_head_block(tq, HEAD_DIM), _head_block(None, HEAD_DIM)(seq), _head_block(None, HEAD_DIM)(seq)],
        out_specs=_head_block(tq, HEAD_DIM),
        out_shape=jax.ShapeDtypeStruct((batch, N_HEADS, seq, HEAD_DIM), BF16),
        scratch_shapes=[pltpu.VMEM((tq, 1), F32), pltpu.VMEM((tq, HEAD_DIM), F32)],
        compiler_params=_params("parallel", "arbitrary"),
        name="stick_breaking",
    )(q, k, v)


def _softmax_step(q, k_ref, v_ref, keys, rs, m_ref, acc_ref):
    t = ATT_TILE
    if m_ref is None:
        width = t * len(keys.starts)
        oldest = keys.starts[-1]
        s = _dot_nt(q, k_ref[pl.ds(oldest, width), :])
        if keys.limit is not None:
            cmr = _col_minus_row(q.shape[0], keys.row_mod, width)
            seen = cmr <= keys.limit
            if keys.floor is not None:
                seen = seen & (cmr > keys.floor)
            s = jnp.where(seen, s, NEG_INF)
        acc_ref[rs, :] += _dot(jnp.exp2(s).astype(BF16), v_ref[pl.ds(oldest, width), :])
        return
    for start, mask in reversed(list(zip(keys.starts, keys.masks))):
        s = _dot_nt(q, k_ref[pl.ds(start, t), :])
        if mask is not None:
            s = jnp.where(mask, s, NEG_INF)
        v1 = v_ref[pl.ds(start, t), :]
        m_old = m_ref[rs, :]
        mn = jnp.maximum(m_old, jnp.max(s, axis=1, keepdims=True))
        acc_ref[rs, :] = jnp.exp2(m_old - mn) * acc_ref[rs, :] + _dot(jnp.exp2(s - mn).astype(BF16), v1)
        m_ref[rs, :] = mn


def _either_path(bound_ref, m_refs, sweep):
    bounded = bound_ref[0] <= EXP2_LIMIT

    @pl.when(bounded)
    def _():
        sweep((None,) * len(m_refs))

    @pl.when(jnp.logical_not(bounded))
    def _():
        for ref in m_refs:
            ref[...] = jnp.full(ref.shape, NEG_INF, F32)
        sweep(m_refs)


def _normalised(acc):
    return acc[:, :HEAD_DIM] / acc[:, HEAD_DIM:HEAD_DIM + 1]


def _diff_kernel(bound_ref, q_ref, k_ref, v_ref, lam_ref, g_ref, o_ref, q1_ref, q2_ref, m1_ref, m2_ref,
                 a1_ref, a2_ref, *, lam_init, tq):
    i = pl.program_id(1)
    rows = q_ref.shape[0]
    q = q_ref[...]
    lane = lax.broadcasted_iota(jnp.int32, q.shape, 1)
    zero = jnp.zeros_like(q)
    q1_ref[...] = jnp.where(lane < DIFF_DIM, q, zero)
    q2_ref[...] = jnp.where(lane >= DIFF_DIM, q, zero)
    for ref in (a1_ref, a2_ref):
        ref[...] = jnp.zeros_like(ref)

    def sweep(m_refs):
        def step(rs, keys):
            _softmax_step(q1_ref[rs, :], k_ref, v_ref, keys, rs, m_refs[0], a1_ref)
            _softmax_step(q2_ref[rs, :], k_ref, v_ref, keys, rs, m_refs[1], a2_ref)

        _causal_sweep(step, i, rows, tq, False)

    _either_path(bound_ref, (m1_ref, m2_ref), sweep)

    lp = lam_ref[...]
    lam = (jnp.exp(jnp.sum(lp[0:1, :] * lp[1:2, :], axis=1, keepdims=True))
           - jnp.exp(jnp.sum(lp[2:3, :] * lp[3:4, :], axis=1, keepdims=True)) + lam_init)
    o = _normalised(a1_ref[...]) - lam * _normalised(a2_ref[...])
    o = o * lax.rsqrt(jnp.mean(o * o, axis=-1, keepdims=True) + RMS_EPS) * g_ref[...]
    o_ref[...] = (o * (1.0 - lam_init)).astype(o_ref.dtype)


_SCALAR = pl.BlockSpec(memory_space=pltpu.SMEM)


def _diff_attention(bound, q, k, v1, lam_p, subln_g, batch, seq, lam_init):
    tq = _q_block(seq)
    fix = lambda bh, i: (0, 0)
    return pl.pallas_call(
        functools.partial(_diff_kernel, lam_init=lam_init, tq=tq),
        grid=(batch * N_HEADS, seq // tq),
        in_specs=[_SCALAR, _head_block(tq, HEAD_DIM), _head_block(None, HEAD_DIM)(seq),
                  _head_block(None, LANES)(seq),
                  pl.BlockSpec((4, DIFF_DIM), fix), pl.BlockSpec((1, HEAD_DIM), fix)],
        out_specs=_head_block(tq, HEAD_DIM),
        out_shape=jax.ShapeDtypeStruct((batch, N_HEADS, seq, HEAD_DIM), BF16),
        scratch_shapes=[pltpu.VMEM((tq, HEAD_DIM), BF16), pltpu.VMEM((tq, HEAD_DIM), BF16),
                        pltpu.VMEM((tq, 1), F32), pltpu.VMEM((tq, 1), F32),
                        pltpu.VMEM((tq, LANES), F32), pltpu.VMEM((tq, LANES), F32)],
        compiler_params=_params("parallel", "arbitrary"),
        name="diff_attention",
    )(bound, q, k, v1, lam_p, subln_g)


def _select_bias(score_t, keep_extra_t, ntop):
    nblk = score_t.shape[0]
    sub = lax.broadcasted_iota(jnp.int32, score_t.shape, 0)
    rank = jnp.zeros(score_t.shape, jnp.int32)
    for b in range(nblk):
        row = score_t[b:b + 1, :]
        ahead = (row > score_t) | ((row == score_t) & (sub > b))
        rank = rank + ahead.astype(jnp.int32)
    keep = rank < ntop
    if keep_extra_t is not None:
        keep = keep_extra_t(keep, sub)
    return jnp.where(keep, 0.0, NEG_INF)


def _bias_lanes(bias_t, lane0):
    nblk, t = bias_t.shape
    parts = []
    if lane0:
        parts.append(jnp.zeros((lane0, t), F32))
    parts.append(bias_t)
    if LANES - lane0 - nblk:
        parts.append(jnp.zeros((LANES - lane0 - nblk, t), F32))
    return jnp.concatenate(parts, axis=0).T


def _moba_gate_kernel(q_ref, k_ref, a_ref, o_ref, km_hi, km_lo, *, nblk):
    i = pl.program_id(1)

    @pl.when(i == 0)
    def _():
        kmean = _dot(a_ref[...], k_ref[...])
        hi, lo = _split_bf16(kmean)
        km_hi[...] = hi
        km_lo[...] = lo

    q = q_ref[...]
    rows = q.shape[0]
    gate = _dot_nt(q, km_hi[...]) + _dot_nt(q, km_lo[...])
    gate_t = gate.T[HEAD_DIM:HEAD_DIM + nblk, :]
    blk = lax.broadcasted_iota(jnp.int32, gate_t.shape, 0)
    own = (i * rows + lax.broadcasted_iota(jnp.int32, gate_t.shape, 1)) // MOBA_BLOCK
    score_t = jnp.where(blk < own, gate_t, NEG_INF)
    ksel = min(MOBA_TOPK, nblk - 1)

    def keep_rule(keep, sub):
        return (keep & (sub < own)) | (sub == own)

    bias_t = _select_bias(score_t, keep_rule, ksel)
    o_ref[...] = (q.astype(F32) + _bias_lanes(bias_t, HEAD_DIM)).astype(o_ref.dtype)


def _moba_gate(q_pad, k_aug, amat, batch, seq):
    tq = _q_block(seq)
    nblk = seq // MOBA_BLOCK
    return pl.pallas_call(
        functools.partial(_moba_gate_kernel, nblk=nblk),
        grid=(batch * N_HEADS, seq // tq),
        in_specs=[_head_block(tq, LANES), _head_block(None, LANES)(seq),
                  pl.BlockSpec((LANES, seq), lambda bh, i: (0, 0))],
        out_specs=_head_block(tq, LANES),
        out_shape=jax.ShapeDtypeStruct((batch, N_HEADS, seq, LANES), BF16),
        scratch_shapes=[pltpu.VMEM((LANES, LANES), BF16), pltpu.VMEM((LANES, LANES), BF16)],
        compiler_params=_params("parallel", "arbitrary"),
        name="moba_gate",
    )(q_pad, k_aug, amat)


def _flash_kernel(bound_ref, q_ref, k_ref, v_ref, o_ref, m_ref, acc_ref, *, tq, window_tiles):
    t = ATT_TILE
    i = pl.program_id(1)
    rows = q_ref.shape[0]
    acc_ref[...] = jnp.zeros_like(acc_ref)

    def sweep(m_refs):
        def step(rs, keys):
            _softmax_step(q_ref[rs, :], k_ref, v_ref, keys, rs, m_refs[0], acc_ref)

        if window_tiles is None:
            _causal_sweep(step, i, rows, tq, False)
            return
        assert tq == t
        everything = slice(0, rows)
        tile = lambda back: pl.multiple_of((i - back) * t, t)
        cmr = _col_minus_row(rows, tq)

        @pl.when(i >= window_tiles)
        def _():
            top = window_tiles * t
            step(everything, _Keys([tile(b) for b in range(window_tiles + 1)],
                                   [cmr <= 0] + [None] * (window_tiles - 1) + [cmr > top - NSA_WINDOW],
                                   top, tq, floor=top - NSA_WINDOW))

        @pl.when(i < window_tiles)
        def _():
            step(everything, _Keys([tile(0)], [cmr <= 0], 0, tq))
            for back in range(1, window_tiles):
                @pl.when(i >= back)
                def _():
                    step(everything, _Keys([tile(back)], [None]))

    _either_path(bound_ref, (m_ref,), sweep)
    o_ref[...] = _normalised(acc_ref[...]).astype(o_ref.dtype)


def _flash(bound, q, k, v1, *, q_map, kv_map, o_map, grid, rows, tq, seq, out_lead, out_dtype, window_tiles,
           name):
    lead = (None,) * (q.ndim - 2)
    klead = (None,) * (k.ndim - 2)
    return pl.pallas_call(
        functools.partial(_flash_kernel, tq=tq, window_tiles=window_tiles),
        grid=grid,
        in_specs=[_SCALAR, pl.BlockSpec(lead + (rows, LANES), q_map),
                  pl.BlockSpec(klead + (seq, LANES), kv_map),
                  pl.BlockSpec(klead + (seq, LANES), kv_map)],
        out_specs=pl.BlockSpec(lead + (rows, HEAD_DIM), o_map),
        out_shape=jax.ShapeDtypeStruct(out_lead + (HEAD_DIM,), out_dtype),
        scratch_shapes=[pltpu.VMEM((rows, 1), F32), pltpu.VMEM((rows, LANES), F32)],
        compiler_params=_params("parallel", "arbitrary"),
        name=name,
    )(bound, q, k, v1)


def _nsa_compress_kernel(tk_ref, tv_ref, pe_ref, wk_ref, wv_ref, g_ref, cos_ref, sa_ref, sb_ref,
                         kc_ref, vc_ref):
    rows = tk_ref.shape[0]

    def compress(tok, pe_lo, pe_hi, w_ref):
        half = w_ref.shape[0] // 2
        a = _dot((tok + pe_lo).astype(BF16), w_ref[0:half, :])
        b = _dot((tok + pe_hi).astype(BF16), w_ref[half:, :])
        return a + pltpu.roll(b, rows - 1, 0)

    kc = compress(tk_ref[...].astype(F32), pe_ref[0:1, :], pe_ref[1:2, :], wk_ref)
    vc = compress(tv_ref[...].astype(F32), pe_ref[2:3, :], pe_ref[3:4, :], wv_ref)
    ms = jnp.sum(kc * kc, axis=-1, keepdims=True) / HEAD_DIM
    kc = kc * lax.rsqrt(ms + RMS_EPS) * g_ref[...]
    kc = _rope_apply(kc, cos_ref[...], sa_ref[...], sb_ref[...], HEAD_DIM // ROPE_FRACTION_DEN // 2)
    kc_ref[...] = kc.astype(kc_ref.dtype)
    vc_ref[...] = vc.astype(vc_ref.dtype)


def _nsa_compress(tk, tv, pe4, wk, wv, gk, tabs, batch):
    rows, width = tk.shape[1], tk.shape[2]
    b3 = lambda b: (b, 0, 0)
    fix = lambda b: (0, 0)
    tok = pl.BlockSpec((None, rows, width), b3)
    out = pl.BlockSpec((None, rows, LANES), b3)
    tab = pl.BlockSpec((rows, LANES), fix)
    return pl.pallas_call(
        _nsa_compress_kernel,
        grid=(batch,),
        in_specs=[tok, tok, pl.BlockSpec((4, width), fix),
                  pl.BlockSpec((2 * width, LANES), fix), pl.BlockSpec((2 * width, LANES), fix),
                  pl.BlockSpec((1, LANES), fix), tab, tab, tab],
        out_specs=[out, out],
        out_shape=[jax.ShapeDtypeStruct((batch, rows, LANES), BF16)] * 2,
        compiler_params=_params("parallel"),
        name="nsa_compress",
    )(tk, tv, pe4, wk, wv, gk, *tabs)


def _nsa_select_kernel(q_ref, kc_ref, vc_ref, m_ref, qa_ref, oc_ref, *, nslc, ntop):
    t = ATT_TILE
    i = pl.program_id(1)
    q = q_ref[...]
    rows = q.shape[0]
    ncmp = kc_ref.shape[0]
    s = _dot_nt(q, kc_ref[...])
    qpos = i * t + (lax.broadcasted_iota(jnp.int32, (rows, ncmp), 0) & (t - 1))
    cmp_end = NSA_CMP_STRIDE * lax.broadcasted_iota(jnp.int32, (rows, ncmp), 1) + (NSA_CMP_LEN - 1)
    mask = cmp_end <= qpos
    s = jnp.where(mask, s, NEG_INF)
    e = jnp.where(mask, jnp.exp2(s - jnp.max(s, axis=1, keepdims=True)), 0.0)
    p = e / jnp.maximum(jnp.sum(e, axis=1, keepdims=True), 1e-30)
    oc_ref[...] = _dot(p.astype(BF16), vc_ref[...])[:, :HEAD_DIM].astype(oc_ref.dtype)

    psum = p[0:t]
    for h in range(1, rows // t):
        psum = psum + p[h * t:(h + 1) * t]
    hi, lo = _split_bf16(psum)
    imp = _dot(hi, m_ref[...]) + _dot(lo, m_ref[...])
    imp_t = imp.T[HEAD_DIM:HEAD_DIM + nslc, :]
    blk = lax.broadcasted_iota(jnp.int32, imp_t.shape, 0)
    cur = (i * t + lax.broadcasted_iota(jnp.int32, imp_t.shape, 1)) // NSA_SLC_BLOCK
    forced = (blk == 0) | (blk == cur) | (blk == cur - 1)
    allowed = blk <= cur
    score_t = jnp.where(allowed, jnp.where(forced, BIG, imp_t), NEG_INF)
    bias = _bias_lanes(_select_bias(score_t, None, ntop), HEAD_DIM)
    qf = q.astype(F32)
    for h in range(rows // t):
        qa_ref[h * t:(h + 1) * t, :] = (qf[h * t:(h + 1) * t] + bias).astype(qa_ref.dtype)


def _nsa_select(q4, kc, vc, mmat, batch, seq):
    t = ATT_TILE
    rows = N_HEADS * t
    ncmp = kc.shape[1]
    nslc = seq // NSA_SLC_BLOCK
    qmap = lambda b, i: (b, i, 0, 0)
    kmap = lambda b, i: (b, 0, 0)
    return pl.pallas_call(
        functools.partial(_nsa_select_kernel, nslc=nslc, ntop=min(NSA_TOPN, nslc)),
        grid=(batch, seq // t),
        in_specs=[pl.BlockSpec((None, None, rows, LANES), qmap),
                  pl.BlockSpec((None, ncmp, LANES), kmap), pl.BlockSpec((None, ncmp, LANES), kmap),
                  pl.BlockSpec((ncmp, LANES), lambda b, i: (0, 0))],
        out_specs=[pl.BlockSpec((None, None, rows, LANES), qmap),
                   pl.BlockSpec((None, None, rows, HEAD_DIM), qmap)],
        out_shape=[jax.ShapeDtypeStruct((batch, seq // t, rows, LANES), BF16),
                   jax.ShapeDtypeStruct((batch, seq // t, rows, HEAD_DIM), F32)],
        compiler_params=_params("parallel", "arbitrary"),
        name="nsa_select",
    )(q4, kc, vc, mmat)


def _merge_kernel(x_ref, g_ref, oa_ref, ob_ref, oc_ref, dc_ref, ds_ref, dw_ref, gl_ref, e_ref,
                  wg_ref, bg_ref, wb_ref, wo_ref, y_ref):
    t = ATT_TILE
    x = x_ref[...]
    xn = x * lax.rsqrt(jnp.mean(x * x, axis=-1, keepdims=True) + RMS_EPS) * g_ref[...]
    xn = xn.astype(BF16)

    def token_major(ref):
        return jnp.concatenate([ref[h] for h in range(N_HEADS)], axis=1)

    def unstack(ref):
        return jnp.concatenate(
            [jnp.concatenate([ref[qt, h * t:(h + 1) * t, :] for h in range(N_HEADS)], axis=1)
             for qt in range(ref.shape[0])], axis=0)

    sig = 1.0 / (1.0 + jnp.exp(-gl_ref[...].astype(F32)))
    hi, lo = _split_bf16(sig)
    gexp = _dot(hi, e_ref[...]) + _dot(lo, e_ref[...])
    w = MIX_WIDTH
    o_d = (gexp[:, 0:w] * unstack(dc_ref) + gexp[:, w:2 * w] * unstack(ds_ref)
           + gexp[:, 2 * w:3 * w] * unstack(dw_ref)).astype(BF16)
    merged = None
    for bi, o in enumerate((token_major(oa_ref), token_major(ob_ref), token_major(oc_ref), o_d)):
        gate = 1.0 / (1.0 + jnp.exp(-(_dot(xn, wg_ref[bi]) + bg_ref[bi])))
        term = gate * _dot(o, wb_ref[bi])
        merged = term if merged is None else merged + term
    y_ref[...] = x + _dot(merged.astype(BF16), wo_ref[...])


def _merge(x2, g, oa, ob, oc, dc, ds, dw, glog, emat, wg, bg, wb, wo, seq, tm):
    n, d = x2.shape
    nst = seq // tm
    t = ATT_TILE
    row = lambda i: (i, 0)
    fix2 = lambda i: (0, 0)
    fix3 = lambda i: (0, 0, 0)
    heads = pl.BlockSpec((None, N_HEADS, tm, HEAD_DIM), lambda i: (i // nst, 0, i % nst, 0))
    stacked = pl.BlockSpec((None, tm // t, N_HEADS * t, HEAD_DIM), lambda i: (i // nst, i % nst, 0, 0))
    return pl.pallas_call(
        _merge_kernel,
        grid=(n // tm,),
        in_specs=[pl.BlockSpec((tm, d), row), _resident((1, d), fix2), heads, heads, heads,
                  stacked, stacked, stacked,
                  pl.BlockSpec((None, tm, LANES), lambda i: (i // nst, i % nst, 0)),
                  _resident((LANES, 3 * MIX_WIDTH), fix2),
                  _resident((4, d, d), fix3), _resident((4, 1, d), fix3),
                  _resident((4, MIX_WIDTH, d), fix3), _resident((d, d), fix2)],
        out_specs=pl.BlockSpec((tm, d), row),
        out_shape=jax.ShapeDtypeStruct((n, d), F32),
        compiler_params=_params("parallel"),
        name="gated_merge",
    )(x2, g, oa, ob, oc, dc, ds, dw, glog, emat, wg, bg, wb, wo)


def _ffn_kernel(x_ref, g_ref, wg_ref, wu_ref, wd_ref, y_ref):
    x = x_ref[...]
    hn = x * lax.rsqrt(jnp.mean(x * x, axis=-1, keepdims=True) + RMS_EPS) * g_ref[...]
    hn = hn.astype(BF16)
    gate = _dot(hn, wg_ref[...])
    up = _dot(hn, wu_ref[...])
    act = gate * (1.0 / (1.0 + jnp.exp(-gate))) * up
    y_ref[...] = x + _dot(act.astype(BF16), wd_ref[...])


def _ffn(x2, g, wg, wu, wd, tm):
    n, d = x2.shape
    f = wg.shape[1]
    row = lambda i: (i, 0)
    fix = lambda i: (0, 0)
    return pl.pallas_call(
        _ffn_kernel,
        grid=(n // tm,),
        in_specs=[pl.BlockSpec((tm, d), row), _resident((1, d), fix), _resident((d, f), fix),
                  _resident((d, f), fix), _resident((f, d), fix)],
        out_specs=pl.BlockSpec((tm, d), row),
        out_shape=jax.ShapeDtypeStruct((n, d), F32),
        compiler_params=_params("parallel"),
        name="swiglu",
    )(x2, g, wg, wu, wd)


def _layer_constants(seq):
    t = ATT_TILE
    pos = jnp.arange(seq, dtype=jnp.int32)
    tabs64 = _rope_tables(pos, HEAD_DIM, GROUP)
    tabs32 = _rope_tables(pos, DIFF_DIM, GROUP)
    ncmp = seq // NSA_CMP_STRIDE
    cmp_end = NSA_CMP_STRIDE * jnp.arange(ncmp, dtype=jnp.int32) + (NSA_CMP_LEN - 1)
    tabs_cmp = _rope_tables(cmp_end, HEAD_DIM, HEAD_DIM)
    tabs_cmp = tuple(jnp.pad(a, ((0, 0), (0, LANES - HEAD_DIM)), constant_values=c)
                     for a, c in zip(tabs_cmp, (1.0, 0.0, 0.0)))

    s = np.arange(seq)
    nblk = seq // MOBA_BLOCK
    nslc = seq // NSA_SLC_BLOCK
    assert nblk <= LANES - HEAD_DIM and nslc <= LANES - HEAD_DIM and seq % t == 0
    moba_hot = np.zeros((seq, LANES - HEAD_DIM), np.float32)
    moba_hot[s, s // MOBA_BLOCK] = 1.0
    slc_hot = np.zeros((seq, LANES - HEAD_DIM), np.float32)
    slc_hot[s, s // NSA_SLC_BLOCK] = 1.0
    amat = np.zeros((LANES, seq), np.float32)
    amat[HEAD_DIM + s // MOBA_BLOCK, s] = 1.0 / MOBA_BLOCK

    nc_real = (seq - NSA_CMP_LEN) // NSA_CMP_STRIDE + 1
    cstart = NSA_CMP_STRIDE * np.arange(ncmp)
    sstart = NSA_SLC_BLOCK * np.arange(nslc)
    overlap = np.clip(np.minimum(cstart[:, None] + NSA_CMP_LEN, sstart[None, :] + NSA_SLC_BLOCK)
                      - np.maximum(cstart[:, None], sstart[None, :]), 0, None)
    mmat = np.zeros((ncmp, LANES), np.float32)
    mmat[:nc_real, HEAD_DIM:HEAD_DIM + nslc] = overlap[:nc_real].astype(np.float32) / NSA_CMP_STRIDE

    emat = np.zeros((LANES, 3 * MIX_WIDTH), np.float32)
    for br in range(3):
        for hd in range(N_HEADS):
            emat[br * N_HEADS + hd,
                 br * MIX_WIDTH + hd * HEAD_DIM: br * MIX_WIDTH + (hd + 1) * HEAD_DIM] = 1.0

    return dict(tabs64=tabs64, tabs32=tabs32, tabs_cmp=tabs_cmp,
                moba_hot=jnp.asarray(moba_hot, BF16), slc_hot=jnp.asarray(slc_hot, BF16),
                amat=jnp.asarray(amat, BF16), mmat=jnp.asarray(mmat, BF16), emat=jnp.asarray(emat, BF16))


def _lane_vectors(diff_qn_g, diff_kn_g, moba_qn_g, moba_kn_g, nsa_qn_g, nsa_kn_g):
    ones = jnp.ones((GROUP,), F32)
    zeros = jnp.zeros((GROUP,), F32)
    sc64_log2 = LOG2_E / math.sqrt(HEAD_DIM)
    sc32_log2 = LOG2_E / math.sqrt(DIFF_DIM)
    quarter = lambda g, k: jnp.concatenate([g if j == k else jnp.ones((HEAD_DIM,), F32) for j in range(4)])
    flag = lambda k: jnp.concatenate([jnp.full((HEAD_DIM,), 1.0 if j == k else 0.0, F32) for j in range(4)])
    gains = [ones / math.sqrt(HEAD_DIM), ones, ones,
             jnp.tile(diff_qn_g, GROUP // DIFF_DIM) * sc32_log2, jnp.tile(diff_kn_g, GROUP // DIFF_DIM), ones,
             jnp.tile(moba_qn_g, N_HEADS) * sc64_log2, jnp.tile(moba_kn_g, N_HEADS), ones,
             jnp.tile(nsa_qn_g, N_HEADS) * sc64_log2,
             quarter(nsa_kn_g[1], 2), quarter(nsa_kn_g[2], 0)]
    flags = [zeros, zeros, zeros, ones, ones, zeros, ones, ones, zeros, ones, flag(2), flag(0)]
    shape = (N_GROUPS, 1, GROUP)
    return jnp.stack(gains).reshape(shape), jnp.stack(flags).reshape(shape), jnp.stack(flags).reshape(shape)


def _row_tile(n, seq, want):
    tm = want if (n % want == 0 and seq % want == 0) else ATT_TILE
    return min(tm, seq)


def _mixers(x2, lw, consts, batch, seq, lam_init):
    n, d = x2.shape
    t = ATT_TILE
    nq = seq // t
    tm = _row_tile(n, seq, 512)

    gain, nflag, rflag = _lane_vectors(lw["diff_qn_g"], lw["diff_kn_g"], lw["moba_qn_g"],
                                       lw["moba_kn_g"], lw["nsa_qn_g"], lw["nsa_kn_g"])
    w_in = jnp.pad(lw["w_in"], ((0, 0), (0, N_GROUPS * GROUP - lw["w_in"].shape[1]))).astype(BF16)
    (sbq, sbk, sbv, dfq, dfk, dfv1, mbq, mbk, mbv1, nsq, kct, vct, ksa, vs1, kwp, vw1, glog) = _inproj(
        x2, lw["attn_norm_g"].reshape(1, d), w_in, gain, nflag, rflag, consts, batch, seq, tm)

    amax = lambda g: jnp.max(jnp.abs(g))
    score_bound = lambda d_head, gq, gk: (math.sqrt(d_head) * LOG2_E * NORM_SLACK * NORM_SLACK
                                          * amax(gq) * amax(gk)).reshape(1).astype(F32)

    o_a = _stick_breaking(sbq, sbk, sbv, batch, seq)
    o_b = _diff_attention(score_bound(DIFF_DIM, lw["diff_qn_g"], lw["diff_kn_g"]), dfq, dfk, dfv1,
                          lw["diff_lam"], lw["diff_subln_g"].reshape(1, HEAD_DIM), batch, seq, lam_init)

    mq_aug = _moba_gate(mbq, mbk, consts["amat"], batch, seq)
    bh_map = lambda bh, i: (bh // N_HEADS, bh % N_HEADS, i, 0)
    bh_full = lambda bh, i: (bh // N_HEADS, bh % N_HEADS, 0, 0)
    tq = _q_block(seq)
    o_c = _flash(score_bound(HEAD_DIM, lw["moba_qn_g"], lw["moba_kn_g"]), mq_aug, mbk, mbv1,
                 q_map=bh_map, kv_map=bh_full, o_map=bh_map,
                 grid=(batch * N_HEADS, seq // tq), rows=tq, tq=tq, seq=seq,
                 out_lead=(batch, N_HEADS, seq), out_dtype=BF16, window_tiles=None, name="moba_attention")

    ncmp = seq // NSA_CMP_STRIDE
    tok = lambda a: a.reshape(batch, ncmp, NSA_CMP_STRIDE * HEAD_DIM)
    pad64 = lambda a: jnp.pad(a, [(0, 0)] * (a.ndim - 1) + [(0, LANES - HEAD_DIM)])
    pe4 = lw["nsa_cmp_pe"].reshape(4, NSA_CMP_STRIDE * HEAD_DIM)
    wc = pad64(lw["nsa_cmp_w"]).astype(BF16)
    gk = pad64(lw["nsa_kn_g"][0].reshape(1, HEAD_DIM))
    kc, vc = _nsa_compress(tok(kct), tok(vct), pe4, wc[0], wc[1], gk, consts["tabs_cmp"], batch)
    q_aug, d_c = _nsa_select(nsq, kc, vc, consts["mmat"], batch, seq)

    rows = N_HEADS * t
    b_map = lambda b, i: (b, i, 0, 0)
    b_full = lambda b, i: (b, 0, 0)
    nsa_flash = functools.partial(_flash, q_map=b_map, kv_map=b_full, o_map=b_map, grid=(batch, nq),
                                  rows=rows, tq=t, seq=seq, out_lead=(batch, nq, rows), out_dtype=F32)
    d_s = nsa_flash(score_bound(HEAD_DIM, lw["nsa_qn_g"], lw["nsa_kn_g"][1]), q_aug, ksa, vs1,
                    window_tiles=None, name="nsa_selected")
    d_w = nsa_flash(score_bound(HEAD_DIM, lw["nsa_qn_g"], lw["nsa_kn_g"][2]), nsq, kwp, vw1,
                    window_tiles=NSA_WINDOW // t, name="nsa_window")
    return glog, o_a, o_b, o_c, d_c, d_s, d_w


def _layer(x2, lw, consts, batch, seq, lam_init):
    n, d = x2.shape
    glog, o_a, o_b, o_c, d_c, d_s, d_w = _mixers(x2, lw, consts, batch, seq, lam_init)
    x2 = _merge(x2, lw["attn_norm_g"].reshape(1, d), o_a, o_b, o_c, d_c, d_s, d_w,
                glog, consts["emat"], lw["w_gate"].astype(BF16), lw["b_gate"].reshape(4, 1, d),
                lw["w_branch"].astype(BF16), lw["w_out"].astype(BF16), seq, _row_tile(n, seq, 512))
    return _ffn(x2, lw["ffn_norm_g"].reshape(1, d), lw["w_ffn_gate"].astype(BF16),
                lw["w_ffn_up"].astype(BF16), lw["w_ffn_down"].astype(BF16), _row_tile(n, seq, 256))


def kernel(x, attn_norm_g, w_in, diff_qn_g, diff_kn_g, diff_lam, diff_subln_g, moba_qn_g, moba_kn_g,
           nsa_qn_g, nsa_kn_g, nsa_cmp_pe, nsa_cmp_w, w_gate, b_gate, w_branch, w_out, ffn_norm_g,
           w_ffn_gate, w_ffn_up, w_ffn_down):
    batch, seq, d = x.shape
    weights = dict(attn_norm_g=attn_norm_g, w_in=w_in, diff_qn_g=diff_qn_g, diff_kn_g=diff_kn_g,
                   diff_lam=diff_lam, diff_subln_g=diff_subln_g, moba_qn_g=moba_qn_g, moba_kn_g=moba_kn_g,
                   nsa_qn_g=nsa_qn_g, nsa_kn_g=nsa_kn_g, nsa_cmp_pe=nsa_cmp_pe, nsa_cmp_w=nsa_cmp_w,
                   w_gate=w_gate, b_gate=b_gate, w_branch=w_branch, w_out=w_out, ffn_norm_g=ffn_norm_g,
                   w_ffn_gate=w_ffn_gate, w_ffn_up=w_ffn_up, w_ffn_down=w_ffn_down)
    consts = _layer_constants(seq)
    x2 = x.reshape(batch * seq, d)
    for layer in range(w_in.shape[0]):
        lw = {k: v[layer] for k, v in weights.items()}
        lam_init = 0.8 - 0.6 * math.exp(-0.3 * layer)
        x2 = _layer(x2, lw, consts, batch, seq, lam_init)
    return x2.reshape(batch, seq, d)
```

```python
import functools
import math
from typing import Any, NamedTuple

import numpy as np
import jax
import jax.numpy as jnp
from jax import lax
from jax.experimental import pallas as pl
from jax.experimental.pallas import tpu as pltpu

HEAD_DIM = 64
N_HEADS = 4
MIX_WIDTH = N_HEADS * HEAD_DIM
ROPE_THETA = 500000.0
ROPE_FRACTION_DEN = 4
DIFF_DIM = HEAD_DIM // 2
MOBA_BLOCK = 256
MOBA_TOPK = 3
NSA_CMP_LEN = 32
NSA_CMP_STRIDE = 16
NSA_SLC_BLOCK = 64
NSA_TOPN = 16
NSA_WINDOW = 512
NEG_INF = -1e30
BIG = 1e30
RMS_EPS = 1e-6

LANES = 128
ATT_TILE = 256
UNROLL = 4
DIAG_ROWS = 512
GROUP = 256
N_GROUPS = 12
VMEM_LIMIT = 56 * 1024 * 1024
LOG2_E = math.log2(math.e)
EXP2_LIMIT = 40.0
NORM_SLACK = 1.05

F32 = jnp.float32
BF16 = jnp.bfloat16

_NT = (((1,), (1,)), ((), ()))


def _dot(a, b):
    return jnp.dot(a, b, preferred_element_type=F32)


def _dot_nt(a, b):
    return lax.dot_general(a, b, _NT, preferred_element_type=F32)


def _split_bf16(x):
    hi = x.astype(BF16)
    lo = (x - hi.astype(F32)).astype(BF16)
    return hi, lo


def _params(*sem):
    return pltpu.CompilerParams(dimension_semantics=sem, vmem_limit_bytes=VMEM_LIMIT)


def _resident(shape, index_map):
    return pl.BlockSpec(shape, index_map, pipeline_mode=pl.Buffered(1))


def _rope_tables(pos, d, width):
    r = d // ROPE_FRACTION_DEN
    half = r // 2
    inv = ROPE_THETA ** (-jnp.arange(half, dtype=F32) * 2.0 / r)
    ang = pos.astype(F32)[:, None] * inv[None, :]
    cos, sin = jnp.cos(ang), jnp.sin(ang)
    n = pos.shape[0]
    one = jnp.ones((n, d - r), F32)
    zero_h = jnp.zeros((n, half), F32)
    zero_t = jnp.zeros((n, d - r), F32)
    cos_g = jnp.concatenate([cos, cos, one], axis=1)
    sa_g = jnp.concatenate([-sin, zero_h, zero_t], axis=1)
    sb_g = jnp.concatenate([zero_h, sin, zero_t], axis=1)
    reps = width // d
    return (jnp.tile(cos_g, (1, reps)), jnp.tile(sa_g, (1, reps)), jnp.tile(sb_g, (1, reps)))


def _group_mean_matrix(d, width):
    g = np.arange(width) // d
    return jnp.asarray((g[:, None] == g[None, :]).astype(np.float32) / d, BF16)


def _rope_apply(y, cos, sa, sb, half):
    w = y.shape[-1]
    return y * cos + pltpu.roll(y, w - half, 1) * sa + pltpu.roll(y, half, 1) * sb


_GROUP_CFG = (
    (None, None), (None, None), (None, None),
    (DIFF_DIM, DIFF_DIM), (DIFF_DIM, DIFF_DIM), (None, None),
    (HEAD_DIM, HEAD_DIM), (HEAD_DIM, HEAD_DIM), (None, None),
    (HEAD_DIM, HEAD_DIM),
    (HEAD_DIM, HEAD_DIM),
    (HEAD_DIM, HEAD_DIM),
)


def _inproj_kernel(x_ref, g_ref, w_ref, gain_ref, nflag_ref, rflag_ref, b64_ref, b32_ref,
                   c64_ref, sa64_ref, sb64_ref, c32_ref, sa32_ref, sb32_ref, mhot_ref, shot_ref,
                   sbq, sbk, sbv, dfq, dfk, dfv, mbq, mbk, mbv, nsq, kct, vct, ksa, vs1, kwp, vw1, glog):
    x = x_ref[...]
    tm = x.shape[0]
    t = ATT_TILE
    hd = HEAD_DIM
    xn = x * lax.rsqrt(jnp.mean(x * x, axis=-1, keepdims=True) + RMS_EPS) * g_ref[...]
    xn = xn.astype(BF16)
    zeros = jnp.zeros((tm, hd), BF16)
    ones_lane = jnp.where(lax.broadcasted_iota(jnp.int32, (tm, hd), 1) == 0, 1.0, 0.0).astype(BF16)

    def project(gi):
        norm, rope = _GROUP_CFG[gi]
        y = _dot(xn, w_ref[:, gi * GROUP:(gi + 1) * GROUP])
        gain = gain_ref[gi]
        if norm is not None:
            bmat = b64_ref[...] if norm == HEAD_DIM else b32_ref[...]
            ms = _dot((y * y).astype(BF16), bmat)
            y = y * jnp.where(nflag_ref[gi] > 0, lax.rsqrt(ms + RMS_EPS) * gain, gain)
        else:
            y = y * gain
        if rope is not None:
            if rope == HEAD_DIM:
                cos, sa, sb = c64_ref[...], sa64_ref[...], sb64_ref[...]
            else:
                cos, sa, sb = c32_ref[...], sa32_ref[...], sb32_ref[...]
            rf = rflag_ref[gi]
            cos = jnp.where(rf > 0, cos, 1.0)
            y = _rope_apply(y, cos, sa * rf, sb * rf, rope // ROPE_FRACTION_DEN // 2)
        return y

    def quarters(y):
        return [y[:, h * hd:(h + 1) * hd].astype(BF16) for h in range(N_HEADS)]

    def put_heads(ref, gi, extra):
        for h, piece in enumerate(quarters(project(gi))):
            if extra is None:
                ref[h] = piece
            else:
                ref[h, :, 0:hd] = piece
                ref[h, :, hd:] = extra

    put_heads(sbq, 0, None)
    put_heads(sbk, 1, None)
    put_heads(sbv, 2, None)
    put_heads(dfq, 3, None)
    put_heads(dfk, 4, None)
    put_heads(dfv, 5, ones_lane)
    put_heads(mbq, 6, zeros)
    put_heads(mbk, 7, mhot_ref[...])
    put_heads(mbv, 8, ones_lane)

    for h, piece in enumerate(quarters(project(9))):
        for qt in range(tm // t):
            nsq[qt, h * t:(h + 1) * t, 0:hd] = piece[qt * t:(qt + 1) * t]
            nsq[qt, h * t:(h + 1) * t, hd:] = zeros[0:t]

    kc_tok, vc_tok, k_slc, v_slc = quarters(project(10))
    kct[...] = kc_tok
    vct[...] = vc_tok
    ksa[:, 0:hd] = k_slc
    ksa[:, hd:] = shot_ref[...]
    vs1[:, 0:hd] = v_slc
    vs1[:, hd:] = ones_lane

    y = project(11)
    kwp[:, 0:hd] = y[:, 0:hd].astype(BF16)
    kwp[:, hd:] = zeros
    vw1[:, 0:hd] = y[:, hd:2 * hd].astype(BF16)
    vw1[:, hd:] = ones_lane
    glog[...] = y[:, 2 * hd:].astype(BF16)


def _inproj(x2, g, w, gain, nflag, rflag, consts, batch, seq, tm):
    n, d = x2.shape
    ncols = N_GROUPS * GROUP
    nst = seq // tm
    t = ATT_TILE
    row = lambda i: (i, 0)
    fix2 = lambda i: (0, 0)
    fix3 = lambda i: (0, 0, 0)
    tab = lambda i: (i % nst, 0)
    vec = _resident((N_GROUPS, 1, GROUP), fix3)
    tspec = pl.BlockSpec((tm, GROUP), tab)
    hspec = pl.BlockSpec((tm, HEAD_DIM), tab)

    def heads_out(width):
        return (pl.BlockSpec((None, N_HEADS, tm, width), lambda i: (i // nst, 0, i % nst, 0)),
                jax.ShapeDtypeStruct((batch, N_HEADS, seq, width), BF16))

    def tokens_out(width):
        return (pl.BlockSpec((None, tm, width), lambda i: (i // nst, i % nst, 0)),
                jax.ShapeDtypeStruct((batch, seq, width), BF16))

    stacked_q = (pl.BlockSpec((None, tm // t, N_HEADS * t, LANES), lambda i: (i // nst, i % nst, 0, 0)),
                 jax.ShapeDtypeStruct((batch, seq // t, N_HEADS * t, LANES), BF16))
    outs = ([heads_out(HEAD_DIM)] * 5 + [heads_out(LANES)] * 4 + [stacked_q]
            + [tokens_out(HEAD_DIM)] * 2 + [tokens_out(LANES)] * 5)
    return pl.pallas_call(
        _inproj_kernel,
        grid=(n // tm,),
        in_specs=[pl.BlockSpec((tm, d), row), _resident((1, d), fix2), _resident((d, ncols), fix2),
                  vec, vec, vec, _resident((GROUP, GROUP), fix2), _resident((GROUP, GROUP), fix2),
                  tspec, tspec, tspec, tspec, tspec, tspec, hspec, hspec],
        out_specs=[o[0] for o in outs],
        out_shape=[o[1] for o in outs],
        compiler_params=_params("parallel"),
        name="inproj",
    )(x2, g, w, gain, nflag, rflag, _group_mean_matrix(HEAD_DIM, GROUP),
      _group_mean_matrix(DIFF_DIM, GROUP), *consts["tabs64"], *consts["tabs32"],
      consts["moba_hot"], consts["slc_hot"])


def _col_minus_row(nrows, row_mod=None, width=ATT_TILE):
    shape = (nrows, width)
    row = lax.broadcasted_iota(jnp.int32, shape, 0)
    if row_mod is not None:
        row = row & (row_mod - 1)
    return lax.broadcasted_iota(jnp.int32, shape, 1) - row


class _Keys(NamedTuple):
    starts: list
    masks: list
    limit: Any = None
    row_mod: Any = None
    floor: Any = None


def _causal_sweep(step, i, rows, tq, strict):
    t = ATT_TILE
    per_block = tq // t
    base = i * per_block
    tile = lambda j: pl.multiple_of(j * t, t)
    visible = (lambda cmr, lim: cmr < lim) if strict else (lambda cmr, lim: cmr <= lim)
    edge = 1 if strict else 0
    if tq == t:
        step(slice(0, rows), _Keys([tile(base)], [visible(_col_minus_row(rows, tq), 0)], -edge, tq))
    else:
        assert rows == tq
        chunk = min(DIAG_ROWS, rows)
        cmr = _col_minus_row(chunk)
        for r0 in range(0, rows, chunk):
            starts, masks = [], []
            for d in range((r0 + chunk - 1) // t, -1, -1):
                starts.append(tile(base + d))
                some_key_ahead = d * t + t - 1 > r0 - edge
                masks.append(visible(cmr, r0 - d * t) if some_key_ahead else None)
            step(slice(r0, r0 + chunk), _Keys(starts, masks, r0 - edge))

    groups = base // UNROLL

    def body(g, carry):
        newest = base - 1 - g * UNROLL
        step(slice(0, rows), _Keys([tile(newest - u) for u in range(UNROLL)], [None] * UNROLL))
        return carry

    lax.fori_loop(0, groups, body, 0)
    if per_block % UNROLL == 0:
        return
    left = base - groups * UNROLL
    for r in range(1, UNROLL):
        @pl.when(left == r)
        def _():
            step(slice(0, rows), _Keys([tile(r - 1 - u) for u in range(r)], [None] * r))


def _sb_kernel(q_ref, k_ref, v_ref, o_ref, carry_ref, acc_ref, *, tq):
    t = ATT_TILE
    i = pl.program_id(1)
    rows = q_ref.shape[0]
    r = lax.broadcasted_iota(jnp.int32, (t, t), 0)
    c = lax.broadcasted_iota(jnp.int32, (t, t), 1)
    suffix = (r > c).astype(BF16)

    def step(rs, keys):
        q = q_ref[rs, :]
        carry = carry_ref[rs, :]
        pv = None
        for start, mask in zip(keys.starts, keys.masks):
            z = _dot_nt(q, k_ref[pl.ds(start, t), :])
            sp = jnp.maximum(z, 0.0) + jnp.log(1.0 + jnp.exp2(jnp.abs(z) * (-LOG2_E)))
            if mask is not None:
                sp = jnp.where(mask, sp, 0.0)
            later = _dot(sp.astype(BF16), suffix)
            a = jnp.exp2((z - sp - later - carry) * LOG2_E)
            if mask is not None:
                a = jnp.where(mask, a, 0.0)
            d = _dot(a.astype(BF16), v_ref[pl.ds(start, t), :])
            pv = d if pv is None else pv + d
            carry = carry + (later[:, 0:1] + sp[:, 0:1])
        acc_ref[rs, :] += pv
        carry_ref[rs, :] = carry

    carry_ref[...] = jnp.zeros_like(carry_ref)
    acc_ref[...] = jnp.zeros_like(acc_ref)
    _causal_sweep(step, i, rows, tq, True)
    o_ref[...] = acc_ref[...].astype(o_ref.dtype)


def _q_block(seq):
    return min(8 * ATT_TILE, seq)


def _head_block(rows, width):
    if rows is None:
        return lambda seq: pl.BlockSpec((None, None, seq, width),
                                        lambda bh, i: (bh // N_HEADS, bh % N_HEADS, 0, 0))
    return pl.BlockSpec((None, None, rows, width), lambda bh, i: (bh // N_HEADS, bh % N_HEADS, i, 0))


def _stick_breaking(q, k, v, batch, seq):
    tq = _q_block(seq)
    return pl.pallas_call(
        functools.partial(_sb_kernel, tq=tq),
        grid=(batch * N_HEADS, seq // tq),
        in_specs=[_head_block(tq, HEAD_DIM), _head_block(None, HEAD_DIM)(seq), _head_block(None, HEAD_DIM)(seq)],
        out_specs=_head_block(tq, HEAD_DIM),
        out_shape=jax.ShapeDtypeStruct((batch, N_HEADS, seq, HEAD_DIM), BF16),
        scratch_shapes=[pltpu.VMEM((tq, 1), F32), pltpu.VMEM((tq, HEAD_DIM), F32)],
        compiler_params=_params("parallel", "arbitrary"),
        name="stick_breaking",
    )(q, k, v)


def _softmax_step(q, k_ref, v_ref, keys, rs, m_ref, acc_ref):
    t = ATT_TILE
    if m_ref is None:
        width = t * len(keys.starts)
        oldest = keys.starts[-1]
        s = _dot_nt(q, k_ref[pl.ds(oldest, width), :])
        if keys.limit is not None:
            cmr = _col_minus_row(q.shape[0], keys.row_mod, width)
            seen = cmr <= keys.limit
            if keys.floor is not None:
                seen = seen & (cmr > keys.floor)
            s = jnp.where(seen, s, NEG_INF)
        acc_ref[rs, :] += _dot(jnp.exp2(s).astype(BF16), v_ref[pl.ds(oldest, width), :])
        return
    for start, mask in reversed(list(zip(keys.starts, keys.masks))):
        s = _dot_nt(q, k_ref[pl.ds(start, t), :])
        if mask is not None:
            s = jnp.where(mask, s, NEG_INF)
        v1 = v_ref[pl.ds(start, t), :]
        m_old = m_ref[rs, :]
        mn = jnp.maximum(m_old, jnp.max(s, axis=1, keepdims=True))
        acc_ref[rs, :] = jnp.exp2(m_old - mn) * acc_ref[rs, :] + _dot(jnp.exp2(s - mn).astype(BF16), v1)
        m_ref[rs, :] = mn


def _either_path(bound_ref, m_refs, sweep):
    bounded = bound_ref[0] <= EXP2_LIMIT

    @pl.when(bounded)
    def _():
        sweep((None,) * len(m_refs))

    @pl.when(jnp.logical_not(bounded))
    def _():
        for ref in m_refs:
            ref[...] = jnp.full(ref.shape, NEG_INF, F32)
        sweep(m_refs)


def _normalised(acc):
    return acc[:, :HEAD_DIM] / acc[:, HEAD_DIM:HEAD_DIM + 1]


def _diff_kernel(bound_ref, q_ref, k_ref, v_ref, lam_ref, g_ref, o_ref, q1_ref, q2_ref, m1_ref, m2_ref,
                 a1_ref, a2_ref, *, lam_init, tq):
    i = pl.program_id(1)
    rows = q_ref.shape[0]
    q = q_ref[...]
    lane = lax.broadcasted_iota(jnp.int32, q.shape, 1)
    zero = jnp.zeros_like(q)
    q1_ref[...] = jnp.where(lane < DIFF_DIM, q, zero)
    q2_ref[...] = jnp.where(lane >= DIFF_DIM, q, zero)
    for ref in (a1_ref, a2_ref):
        ref[...] = jnp.zeros_like(ref)

    def sweep(m_refs):
        def step(rs, keys):
            _softmax_step(q1_ref[rs, :], k_ref, v_ref, keys, rs, m_refs[0], a1_ref)
            _softmax_step(q2_ref[rs, :], k_ref, v_ref, keys, rs, m_refs[1], a2_ref)

        _causal_sweep(step, i, rows, tq, False)

    _either_path(bound_ref, (m1_ref, m2_ref), sweep)

    lp = lam_ref[...]
    lam = (jnp.exp(jnp.sum(lp[0:1, :] * lp[1:2, :], axis=1, keepdims=True))
           - jnp.exp(jnp.sum(lp[2:3, :] * lp[3:4, :], axis=1, keepdims=True)) + lam_init)
    o = _normalised(a1_ref[...]) - lam * _normalised(a2_ref[...])
    o = o * lax.rsqrt(jnp.mean(o * o, axis=-1, keepdims=True) + RMS_EPS) * g_ref[...]
    o_ref[...] = (o * (1.0 - lam_init)).astype(o_ref.dtype)


_SCALAR = pl.BlockSpec(memory_space=pltpu.SMEM)


def _diff_attention(bound, q, k, v1, lam_p, subln_g, batch, seq, lam_init):
    tq = _q_block(seq)
    fix = lambda bh, i: (0, 0)
    return pl.pallas_call(
        functools.partial(_diff_kernel, lam_init=lam_init, tq=tq),
        grid=(batch * N_HEADS, seq // tq),
        in_specs=[_SCALAR, _head_block(tq, HEAD_DIM), _head_block(None, HEAD_DIM)(seq),
                  _head_block(None, LANES)(seq),
                  pl.BlockSpec((4, DIFF_DIM), fix), pl.BlockSpec((1, HEAD_DIM), fix)],
        out_specs=_head_block(tq, HEAD_DIM),
        out_shape=jax.ShapeDtypeStruct((batch, N_HEADS, seq, HEAD_DIM), BF16),
        scratch_shapes=[pltpu.VMEM((tq, HEAD_DIM), BF16), pltpu.VMEM((tq, HEAD_DIM), BF16),
                        pltpu.VMEM((tq, 1), F32), pltpu.VMEM((tq, 1), F32),
                        pltpu.VMEM((tq, LANES), F32), pltpu.VMEM((tq, LANES), F32)],
        compiler_params=_params("parallel", "arbitrary"),
        name="diff_attention",
    )(bound, q, k, v1, lam_p, subln_g)


def _select_bias(score_t, keep_extra_t, ntop):
    nblk = score_t.shape[0]
    sub = lax.broadcasted_iota(jnp.int32, score_t.shape, 0)
    rank = jnp.zeros(score_t.shape, jnp.int32)
    for b in range(nblk):
        row = score_t[b:b + 1, :]
        ahead = (row > score_t) | ((row == score_t) & (sub > b))
        rank = rank + ahead.astype(jnp.int32)
    keep = rank < ntop
    if keep_extra_t is not None:
        keep = keep_extra_t(keep, sub)
    return jnp.where(keep, 0.0, NEG_INF)


def _bias_lanes(bias_t, lane0):
    nblk, t = bias_t.shape
    parts = []
    if lane0:
        parts.append(jnp.zeros((lane0, t), F32))
    parts.append(bias_t)
    if LANES - lane0 - nblk:
        parts.append(jnp.zeros((LANES - lane0 - nblk, t), F32))
    return jnp.concatenate(parts, axis=0).T


def _moba_gate_body(q_ref, k_ref, a_ref, o_ref, km_hi, km_lo, i, nblk):
    @pl.when(i == 0)
    def _():
        kmean = _dot(a_ref[...], k_ref[...])
        hi, lo = _split_bf16(kmean)
        km_hi[...] = hi
        km_lo[...] = lo

    q = q_ref[...]
    rows = q.shape[0]
    gate = _dot_nt(q, km_hi[...]) + _dot_nt(q, km_lo[...])
    gate_t = gate.T[HEAD_DIM:HEAD_DIM + nblk, :]
    blk = lax.broadcasted_iota(jnp.int32, gate_t.shape, 0)
    own = (i * rows + lax.broadcasted_iota(jnp.int32, gate_t.shape, 1)) // MOBA_BLOCK
    score_t = jnp.where(blk < own, gate_t, NEG_INF)
    ksel = min(MOBA_TOPK, nblk - 1)

    def keep_rule(keep, sub):
        return (keep & (sub < own)) | (sub == own)

    bias_t = _select_bias(score_t, keep_rule, ksel)
    o_ref[...] = (q.astype(F32) + _bias_lanes(bias_t, HEAD_DIM)).astype(o_ref.dtype)


def _flash_body(bound_ref, q_ref, k_ref, v_ref, o_ref, m_ref, acc_ref, i, tq, window_tiles):
    t = ATT_TILE
    rows = q_ref.shape[0]
    acc_ref[...] = jnp.zeros_like(acc_ref)

    def sweep(m_refs):
        def step(rs, keys):
            _softmax_step(q_ref[rs, :], k_ref, v_ref, keys, rs, m_refs[0], acc_ref)

        if window_tiles is None:
            _causal_sweep(step, i, rows, tq, False)
            return
        assert tq == t
        everything = slice(0, rows)
        tile = lambda back: pl.multiple_of((i - back) * t, t)
        cmr = _col_minus_row(rows, tq)

        @pl.when(i >= window_tiles)
        def _():
            top = window_tiles * t
            step(everything, _Keys([tile(b) for b in range(window_tiles + 1)],
                                   [cmr <= 0] + [None] * (window_tiles - 1) + [cmr > top - NSA_WINDOW],
                                   top, tq, floor=top - NSA_WINDOW))

        @pl.when(i < window_tiles)
        def _():
            step(everything, _Keys([tile(0)], [cmr <= 0], 0, tq))
            for back in range(1, window_tiles):
                @pl.when(i >= back)
                def _():
                    step(everything, _Keys([tile(back)], [None]))

    _either_path(bound_ref, (m_ref,), sweep)
    o_ref[...] = _normalised(acc_ref[...]).astype(o_ref.dtype)


def _moba_kernel(bound_ref, q_ref, k_ref, v_ref, a_ref, o_ref, km_hi, km_lo, qa_ref, m_ref, acc_ref, *,
                 nblk, tq):
    i = pl.program_id(1)
    _moba_gate_body(q_ref, k_ref, a_ref, qa_ref, km_hi, km_lo, i, nblk)
    _flash_body(bound_ref, qa_ref, k_ref, v_ref, o_ref, m_ref, acc_ref, i, tq, None)


def _moba_attention(bound, q_pad, k_aug, v1, amat, batch, seq):
    tq = _q_block(seq)
    return pl.pallas_call(
        functools.partial(_moba_kernel, nblk=seq // MOBA_BLOCK, tq=tq),
        grid=(batch * N_HEADS, seq // tq),
        in_specs=[_SCALAR, _head_block(tq, LANES), _head_block(None, LANES)(seq), _head_block(None, LANES)(seq),
                  pl.BlockSpec((LANES, seq), lambda bh, i: (0, 0))],
        out_specs=_head_block(tq, HEAD_DIM),
        out_shape=jax.ShapeDtypeStruct((batch, N_HEADS, seq, HEAD_DIM), BF16),
        scratch_shapes=[pltpu.VMEM((LANES, LANES), BF16), pltpu.VMEM((LANES, LANES), BF16),
                        pltpu.VMEM((tq, LANES), BF16), pltpu.VMEM((tq, 1), F32), pltpu.VMEM((tq, LANES), F32)],
        compiler_params=_params("parallel", "arbitrary"),
        name="moba_attention",
    )(bound, q_pad, k_aug, v1, amat)


def _nsa_pair_kernel(bound_s_ref, bound_w_ref, q_ref, ks_ref, vs_ref, kw_ref, vw_ref, os_ref, ow_ref,
                     m_ref, acc_ref, *, window_tiles):
    i = pl.program_id(1)
    t = ATT_TILE
    _flash_body(bound_s_ref, q_ref, ks_ref, vs_ref, os_ref, m_ref, acc_ref, i, t, None)
    _flash_body(bound_w_ref, q_ref, kw_ref, vw_ref, ow_ref, m_ref, acc_ref, i, t, window_tiles)


def _nsa_pair(bound_s, bound_w, q_aug, ks, vs1, kw, vw1, batch, seq):
    t = ATT_TILE
    rows = N_HEADS * t
    nq = seq // t
    qmap = lambda b, i: (b, i, 0, 0)
    kvmap = lambda b, i: (b, 0, 0)
    kv = pl.BlockSpec((None, seq, LANES), kvmap)
    out = pl.BlockSpec((None, None, rows, HEAD_DIM), qmap)
    shape = jax.ShapeDtypeStruct((batch, nq, rows, HEAD_DIM), F32)
    return pl.pallas_call(
        functools.partial(_nsa_pair_kernel, window_tiles=NSA_WINDOW // t),
        grid=(batch, nq),
        in_specs=[_SCALAR, _SCALAR, pl.BlockSpec((None, None, rows, LANES), qmap), kv, kv, kv, kv],
        out_specs=[out, out],
        out_shape=[shape, shape],
        scratch_shapes=[pltpu.VMEM((rows, 1), F32), pltpu.VMEM((rows, LANES), F32)],
        compiler_params=_params("parallel", "arbitrary"),
        name="nsa_selected_window",
    )(bound_s, bound_w, q_aug, ks, vs1, kw, vw1)


def _nsa_compress_kernel(tk_ref, tv_ref, pe_ref, wk_ref, wv_ref, g_ref, cos_ref, sa_ref, sb_ref,
                         kc_ref, vc_ref):
    rows = tk_ref.shape[0]

    def compress(tok, pe_lo, pe_hi, w_ref):
        half = w_ref.shape[0] // 2
        a = _dot((tok + pe_lo).astype(BF16), w_ref[0:half, :])
        b = _dot((tok + pe_hi).astype(BF16), w_ref[half:, :])
        return a + pltpu.roll(b, rows - 1, 0)

    kc = compress(tk_ref[...].astype(F32), pe_ref[0:1, :], pe_ref[1:2, :], wk_ref)
    vc = compress(tv_ref[...].astype(F32), pe_ref[2:3, :], pe_ref[3:4, :], wv_ref)
    ms = jnp.sum(kc * kc, axis=-1, keepdims=True) / HEAD_DIM
    kc = kc * lax.rsqrt(ms + RMS_EPS) * g_ref[...]
    kc = _rope_apply(kc, cos_ref[...], sa_ref[...], sb_ref[...], HEAD_DIM // ROPE_FRACTION_DEN // 2)
    kc_ref[...] = kc.astype(kc_ref.dtype)
    vc_ref[...] = vc.astype(vc_ref.dtype)


def _nsa_compress(tk, tv, pe4, wk, wv, gk, tabs, batch):
    rows, width = tk.shape[1], tk.shape[2]
    b3 = lambda b: (b, 0, 0)
    fix = lambda b: (0, 0)
    tok = pl.BlockSpec((None, rows, width), b3)
    out = pl.BlockSpec((None, rows, LANES), b3)
    tab = pl.BlockSpec((rows, LANES), fix)
    return pl.pallas_call(
        _nsa_compress_kernel,
        grid=(batch,),
        in_specs=[tok, tok, pl.BlockSpec((4, width), fix),
                  pl.BlockSpec((2 * width, LANES), fix), pl.BlockSpec((2 * width, LANES), fix),
                  pl.BlockSpec((1, LANES), fix), tab, tab, tab],
        out_specs=[out, out],
        out_shape=[jax.ShapeDtypeStruct((batch, rows, LANES), BF16)] * 2,
        compiler_params=_params("parallel"),
        name="nsa_compress",
    )(tk, tv, pe4, wk, wv, gk, *tabs)


def _nsa_select_kernel(q_ref, kc_ref, vc_ref, m_ref, qa_ref, oc_ref, *, nslc, ntop):
    t = ATT_TILE
    i = pl.program_id(1)
    q = q_ref[...]
    rows = q.shape[0]
    ncmp = kc_ref.shape[0]
    s = _dot_nt(q, kc_ref[...])
    qpos = i * t + (lax.broadcasted_iota(jnp.int32, (rows, ncmp), 0) & (t - 1))
    cmp_end = NSA_CMP_STRIDE * lax.broadcasted_iota(jnp.int32, (rows, ncmp), 1) + (NSA_CMP_LEN - 1)
    mask = cmp_end <= qpos
    s = jnp.where(mask, s, NEG_INF)
    e = jnp.where(mask, jnp.exp2(s - jnp.max(s, axis=1, keepdims=True)), 0.0)
    p = e / jnp.maximum(jnp.sum(e, axis=1, keepdims=True), 1e-30)
    oc_ref[...] = _dot(p.astype(BF16), vc_ref[...])[:, :HEAD_DIM].astype(oc_ref.dtype)

    psum = p[0:t]
    for h in range(1, rows // t):
        psum = psum + p[h * t:(h + 1) * t]
    hi, lo = _split_bf16(psum)
    imp = _dot(hi, m_ref[...]) + _dot(lo, m_ref[...])
    imp_t = imp.T[HEAD_DIM:HEAD_DIM + nslc, :]
    blk = lax.broadcasted_iota(jnp.int32, imp_t.shape, 0)
    cur = (i * t + lax.broadcasted_iota(jnp.int32, imp_t.shape, 1)) // NSA_SLC_BLOCK
    forced = (blk == 0) | (blk == cur) | (blk == cur - 1)
    allowed = blk <= cur
    score_t = jnp.where(allowed, jnp.where(forced, BIG, imp_t), NEG_INF)
    bias = _bias_lanes(_select_bias(score_t, None, ntop), HEAD_DIM)
    qf = q.astype(F32)
    for h in range(rows // t):
        qa_ref[h * t:(h + 1) * t, :] = (qf[h * t:(h + 1) * t] + bias).astype(qa_ref.dtype)


def _nsa_select(q4, kc, vc, mmat, batch, seq):
    t = ATT_TILE
    rows = N_HEADS * t
    ncmp = kc.shape[1]
    nslc = seq // NSA_SLC_BLOCK
    qmap = lambda b, i: (b, i, 0, 0)
    kmap = lambda b, i: (b, 0, 0)
    return pl.pallas_call(
        functools.partial(_nsa_select_kernel, nslc=nslc, ntop=min(NSA_TOPN, nslc)),
        grid=(batch, seq // t),
        in_specs=[pl.BlockSpec((None, None, rows, LANES), qmap),
                  pl.BlockSpec((None, ncmp, LANES), kmap), pl.BlockSpec((None, ncmp, LANES), kmap),
                  pl.BlockSpec((ncmp, LANES), lambda b, i: (0, 0))],
        out_specs=[pl.BlockSpec((None, None, rows, LANES), qmap),
                   pl.BlockSpec((None, None, rows, HEAD_DIM), qmap)],
        out_shape=[jax.ShapeDtypeStruct((batch, seq // t, rows, LANES), BF16),
                   jax.ShapeDtypeStruct((batch, seq // t, rows, HEAD_DIM), F32)],
        compiler_params=_params("parallel", "arbitrary"),
        name="nsa_select",
    )(q4, kc, vc, mmat)


def _merge_kernel(x_ref, g_ref, oa_ref, ob_ref, oc_ref, dc_ref, ds_ref, dw_ref, gl_ref, e_ref,
                  wg_ref, bg_ref, wb_ref, wo_ref, y_ref):
    t = ATT_TILE
    x = x_ref[...]
    xn = x * lax.rsqrt(jnp.mean(x * x, axis=-1, keepdims=True) + RMS_EPS) * g_ref[...]
    xn = xn.astype(BF16)

    def token_major(ref):
        return jnp.concatenate([ref[h] for h in range(N_HEADS)], axis=1)

    def unstack(ref):
        return jnp.concatenate(
            [jnp.concatenate([ref[qt, h * t:(h + 1) * t, :] for h in range(N_HEADS)], axis=1)
             for qt in range(ref.shape[0])], axis=0)

    sig = 1.0 / (1.0 + jnp.exp(-gl_ref[...].astype(F32)))
    hi, lo = _split_bf16(sig)
    gexp = _dot(hi, e_ref[...]) + _dot(lo, e_ref[...])
    w = MIX_WIDTH
    o_d = (gexp[:, 0:w] * unstack(dc_ref) + gexp[:, w:2 * w] * unstack(ds_ref)
           + gexp[:, 2 * w:3 * w] * unstack(dw_ref)).astype(BF16)
    merged = None
    for bi, o in enumerate((token_major(oa_ref), token_major(ob_ref), token_major(oc_ref), o_d)):
        gate = 1.0 / (1.0 + jnp.exp(-(_dot(xn, wg_ref[bi]) + bg_ref[bi])))
        term = gate * _dot(o, wb_ref[bi])
        merged = term if merged is None else merged + term
    y_ref[...] = x + _dot(merged.astype(BF16), wo_ref[...])


def _merge(x2, g, oa, ob, oc, dc, ds, dw, glog, emat, wg, bg, wb, wo, seq, tm):
    n, d = x2.shape
    nst = seq // tm
    t = ATT_TILE
    row = lambda i: (i, 0)
    fix2 = lambda i: (0, 0)
    fix3 = lambda i: (0, 0, 0)
    heads = pl.BlockSpec((None, N_HEADS, tm, HEAD_DIM), lambda i: (i // nst, 0, i % nst, 0))
    stacked = pl.BlockSpec((None, tm // t, N_HEADS * t, HEAD_DIM), lambda i: (i // nst, i % nst, 0, 0))
    return pl.pallas_call(
        _merge_kernel,
        grid=(n // tm,),
        in_specs=[pl.BlockSpec((tm, d), row), _resident((1, d), fix2), heads, heads, heads,
                  stacked, stacked, stacked,
                  pl.BlockSpec((None, tm, LANES), lambda i: (i // nst, i % nst, 0)),
                  _resident((LANES, 3 * MIX_WIDTH), fix2),
                  _resident((4, d, d), fix3), _resident((4, 1, d), fix3),
                  _resident((4, MIX_WIDTH, d), fix3), _resident((d, d), fix2)],
        out_specs=pl.BlockSpec((tm, d), row),
        out_shape=jax.ShapeDtypeStruct((n, d), F32),
        compiler_params=_params("parallel"),
        name="gated_merge",
    )(x2, g, oa, ob, oc, dc, ds, dw, glog, emat, wg, bg, wb, wo)


def _ffn_kernel(x_ref, g_ref, wg_ref, wu_ref, wd_ref, y_ref):
    x = x_ref[...]
    hn = x * lax.rsqrt(jnp.mean(x * x, axis=-1, keepdims=True) + RMS_EPS) * g_ref[...]
    hn = hn.astype(BF16)
    gate = _dot(hn, wg_ref[...])
    up = _dot(hn, wu_ref[...])
    act = gate * (1.0 / (1.0 + jnp.exp(-gate))) * up
    y_ref[...] = x + _dot(act.astype(BF16), wd_ref[...])


def _ffn(x2, g, wg, wu, wd, tm):
    n, d = x2.shape
    f = wg.shape[1]
    row = lambda i: (i, 0)
    fix = lambda i: (0, 0)
    return pl.pallas_call(
        _ffn_kernel,
        grid=(n // tm,),
        in_specs=[pl.BlockSpec((tm, d), row), _resident((1, d), fix), _resident((d, f), fix),
                  _resident((d, f), fix), _resident((f, d), fix)],
        out_specs=pl.BlockSpec((tm, d), row),
        out_shape=jax.ShapeDtypeStruct((n, d), F32),
        compiler_params=_params("parallel"),
        name="swiglu",
    )(x2, g, wg, wu, wd)


def _layer_constants(seq):
    t = ATT_TILE
    pos = jnp.arange(seq, dtype=jnp.int32)
    tabs64 = _rope_tables(pos, HEAD_DIM, GROUP)
    tabs32 = _rope_tables(pos, DIFF_DIM, GROUP)
    ncmp = seq // NSA_CMP_STRIDE
    cmp_end = NSA_CMP_STRIDE * jnp.arange(ncmp, dtype=jnp.int32) + (NSA_CMP_LEN - 1)
    tabs_cmp = _rope_tables(cmp_end, HEAD_DIM, HEAD_DIM)
    tabs_cmp = tuple(jnp.pad(a, ((0, 0), (0, LANES - HEAD_DIM)), constant_values=c)
                     for a, c in zip(tabs_cmp, (1.0, 0.0, 0.0)))

    s = np.arange(seq)
    nblk = seq // MOBA_BLOCK
    nslc = seq // NSA_SLC_BLOCK
    assert nblk <= LANES - HEAD_DIM and nslc <= LANES - HEAD_DIM and seq % t == 0
    moba_hot = np.zeros((seq, LANES - HEAD_DIM), np.float32)
    moba_hot[s, s // MOBA_BLOCK] = 1.0
    slc_hot = np.zeros((seq, LANES - HEAD_DIM), np.float32)
    slc_hot[s, s // NSA_SLC_BLOCK] = 1.0
    amat = np.zeros((LANES, seq), np.float32)
    amat[HEAD_DIM + s // MOBA_BLOCK, s] = 1.0 / MOBA_BLOCK

    nc_real = (seq - NSA_CMP_LEN) // NSA_CMP_STRIDE + 1
    cstart = NSA_CMP_STRIDE * np.arange(ncmp)
    sstart = NSA_SLC_BLOCK * np.arange(nslc)
    overlap = np.clip(np.minimum(cstart[:, None] + NSA_CMP_LEN, sstart[None, :] + NSA_SLC_BLOCK)
                      - np.maximum(cstart[:, None], sstart[None, :]), 0, None)
    mmat = np.zeros((ncmp, LANES), np.float32)
    mmat[:nc_real, HEAD_DIM:HEAD_DIM + nslc] = overlap[:nc_real].astype(np.float32) / NSA_CMP_STRIDE

    emat = np.zeros((LANES, 3 * MIX_WIDTH), np.float32)
    for br in range(3):
        for hd in range(N_HEADS):
            emat[br * N_HEADS + hd,
                 br * MIX_WIDTH + hd * HEAD_DIM: br * MIX_WIDTH + (hd + 1) * HEAD_DIM] = 1.0

    return dict(tabs64=tabs64, tabs32=tabs32, tabs_cmp=tabs_cmp,
                moba_hot=jnp.asarray(moba_hot, BF16), slc_hot=jnp.asarray(slc_hot, BF16),
                amat=jnp.asarray(amat, BF16), mmat=jnp.asarray(mmat, BF16), emat=jnp.asarray(emat, BF16))


def _lane_vectors(diff_qn_g, diff_kn_g, moba_qn_g, moba_kn_g, nsa_qn_g, nsa_kn_g):
    ones = jnp.ones((GROUP,), F32)
    zeros = jnp.zeros((GROUP,), F32)
    sc64_log2 = LOG2_E / math.sqrt(HEAD_DIM)
    sc32_log2 = LOG2_E / math.sqrt(DIFF_DIM)
    quarter = lambda g, k: jnp.concatenate([g if j == k else jnp.ones((HEAD_DIM,), F32) for j in range(4)])
    flag = lambda k: jnp.concatenate([jnp.full((HEAD_DIM,), 1.0 if j == k else 0.0, F32) for j in range(4)])
    gains = [ones / math.sqrt(HEAD_DIM), ones, ones,
             jnp.tile(diff_qn_g, GROUP // DIFF_DIM) * sc32_log2, jnp.tile(diff_kn_g, GROUP // DIFF_DIM), ones,
             jnp.tile(moba_qn_g, N_HEADS) * sc64_log2, jnp.tile(moba_kn_g, N_HEADS), ones,
             jnp.tile(nsa_qn_g, N_HEADS) * sc64_log2,
             quarter(nsa_kn_g[1], 2), quarter(nsa_kn_g[2], 0)]
    flags = [zeros, zeros, zeros, ones, ones, zeros, ones, ones, zeros, ones, flag(2), flag(0)]
    shape = (N_GROUPS, 1, GROUP)
    return jnp.stack(gains).reshape(shape), jnp.stack(flags).reshape(shape), jnp.stack(flags).reshape(shape)


def _row_tile(n, seq, want):
    tm = want if (n % want == 0 and seq % want == 0) else ATT_TILE
    return min(tm, seq)


def _mixers(x2, lw, consts, batch, seq, lam_init):
    n, d = x2.shape
    t = ATT_TILE
    tm = _row_tile(n, seq, 512)

    gain, nflag, rflag = _lane_vectors(lw["diff_qn_g"], lw["diff_kn_g"], lw["moba_qn_g"],
                                       lw["moba_kn_g"], lw["nsa_qn_g"], lw["nsa_kn_g"])
    w_in = jnp.pad(lw["w_in"], ((0, 0), (0, N_GROUPS * GROUP - lw["w_in"].shape[1]))).astype(BF16)
    (sbq, sbk, sbv, dfq, dfk, dfv1, mbq, mbk, mbv1, nsq, kct, vct, ksa, vs1, kwp, vw1, glog) = _inproj(
        x2, lw["attn_norm_g"].reshape(1, d), w_in, gain, nflag, rflag, consts, batch, seq, tm)

    amax = lambda g: jnp.max(jnp.abs(g))
    score_bound = lambda d_head, gq, gk: (math.sqrt(d_head) * LOG2_E * NORM_SLACK * NORM_SLACK
                                          * amax(gq) * amax(gk)).reshape(1).astype(F32)

    o_a = _stick_breaking(sbq, sbk, sbv, batch, seq)
    o_b = _diff_attention(score_bound(DIFF_DIM, lw["diff_qn_g"], lw["diff_kn_g"]), dfq, dfk, dfv1,
                          lw["diff_lam"], lw["diff_subln_g"].reshape(1, HEAD_DIM), batch, seq, lam_init)

    o_c = _moba_attention(score_bound(HEAD_DIM, lw["moba_qn_g"], lw["moba_kn_g"]), mbq, mbk, mbv1,
                          consts["amat"], batch, seq)

    ncmp = seq // NSA_CMP_STRIDE
    tok = lambda a: a.reshape(batch, ncmp, NSA_CMP_STRIDE * HEAD_DIM)
    pad64 = lambda a: jnp.pad(a, [(0, 0)] * (a.ndim - 1) + [(0, LANES - HEAD_DIM)])
    pe4 = lw["nsa_cmp_pe"].reshape(4, NSA_CMP_STRIDE * HEAD_DIM)
    wc = pad64(lw["nsa_cmp_w"]).astype(BF16)
    gk = pad64(lw["nsa_kn_g"][0].reshape(1, HEAD_DIM))
    kc, vc = _nsa_compress(tok(kct), tok(vct), pe4, wc[0], wc[1], gk, consts["tabs_cmp"], batch)
    q_aug, d_c = _nsa_select(nsq, kc, vc, consts["mmat"], batch, seq)

    d_s, d_w = _nsa_pair(score_bound(HEAD_DIM, lw["nsa_qn_g"], lw["nsa_kn_g"][1]),
                         score_bound(HEAD_DIM, lw["nsa_qn_g"], lw["nsa_kn_g"][2]),
                         q_aug, ksa, vs1, kwp, vw1, batch, seq)
    return glog, o_a, o_b, o_c, d_c, d_s, d_w


def _layer(x2, lw, consts, batch, seq, lam_init):
    n, d = x2.shape
    glog, o_a, o_b, o_c, d_c, d_s, d_w = _mixers(x2, lw, consts, batch, seq, lam_init)
    x2 = _merge(x2, lw["attn_norm_g"].reshape(1, d), o_a, o_b, o_c, d_c, d_s, d_w,
                glog, consts["emat"], lw["w_gate"].astype(BF16), lw["b_gate"].reshape(4, 1, d),
                lw["w_branch"].astype(BF16), lw["w_out"].astype(BF16), seq, _row_tile(n, seq, 512))
    return _ffn(x2, lw["ffn_norm_g"].reshape(1, d), lw["w_ffn_gate"].astype(BF16),
                lw["w_ffn_up"].astype(BF16), lw["w_ffn_down"].astype(BF16), _row_tile(n, seq, 256))


def kernel(x, attn_norm_g, w_in, diff_qn_g, diff_kn_g, diff_lam, diff_subln_g, moba_qn_g, moba_kn_g,
           nsa_qn_g, nsa_kn_g, nsa_cmp_pe, nsa_cmp_w, w_gate, b_gate, w_branch, w_out, ffn_norm_g,
           w_ffn_gate, w_ffn_up, w_ffn_down):
    batch, seq, d = x.shape
    weights = dict(attn_norm_g=attn_norm_g, w_in=w_in, diff_qn_g=diff_qn_g, diff_kn_g=diff_kn_g,
                   diff_lam=diff_lam, diff_subln_g=diff_subln_g, moba_qn_g=moba_qn_g, moba_kn_g=moba_kn_g,
                   nsa_qn_g=nsa_qn_g, nsa_kn_g=nsa_kn_g, nsa_cmp_pe=nsa_cmp_pe, nsa_cmp_w=nsa_cmp_w,
                   w_gate=w_gate, b_gate=b_gate, w_branch=w_branch, w_out=w_out, ffn_norm_g=ffn_norm_g,
                   w_ffn_gate=w_ffn_gate, w_ffn_up=w_ffn_up, w_ffn_down=w_ffn_down)
    consts = _layer_constants(seq)
    x2 = x.reshape(batch * seq, d)
    for layer in range(w_in.shape[0]):
        lw = {k: v[layer] for k, v in weights.items()}
        lam_init = 0.8 - 0.6 * math.exp(-0.3 * layer)
        x2 = _layer(x2, lw, consts, batch, seq, lam_init)
    return x2.reshape(batch, seq, d)
```
